```python
import math
import jax, jax.numpy as jnp
from jax import lax
import numpy as np

D_MODEL = 1024
BATCH = 4
SEQ = 8192
DEPTH = 4

CHUNK = 64
N_MIXERS = 2
N_A_LAYERS = (DEPTH + 1) // 2
N_B_LAYERS = DEPTH // 2
HEAD_DIM = 64
N_HEADS = D_MODEL // HEAD_DIM
ATTN_DIM = N_HEADS * HEAD_DIM
KV_GROUPS = 4
Q_PER_KV = N_HEADS // KV_GROUPS
IDX_HEADS = 8
IDX_DIM = 64
TOPK_MAX = 256
QBLK = 128
T5_BUCKETS = 32
T5_MAX_DIST = 128
LEFT_CHUNKS = 8
BAND = (LEFT_CHUNKS + 1) * CHUNK
REL_CLIP = 256
D_FF = 2816
CONV_WIDTH = 3
EPS = 1e-6
A_SIZES = (ATTN_DIM, KV_GROUPS * HEAD_DIM, KV_GROUPS * HEAD_DIM,
           IDX_HEADS * IDX_DIM, IDX_DIM, IDX_HEADS)
A_IN = sum(A_SIZES)
A_SPLITS = tuple(int(v) for v in np.cumsum(A_SIZES)[:-1])
B_IN = 3 * ATTN_DIM

kernel_name = "hybrid_dsa_chunkband_convffn"


def rmsnorm(x, g):
    xf = x.astype(jnp.float32)
    y = xf * lax.rsqrt(jnp.mean(xf * xf, axis=-1, keepdims=True) + EPS)
    return (y * g.astype(jnp.float32)).astype(x.dtype)


def rms_unit(x):
    xf = x.astype(jnp.float32)
    return (xf * lax.rsqrt(jnp.mean(xf * xf, axis=-1, keepdims=True) + EPS)).astype(x.dtype)


def t5_bucket(rel):
    half = T5_BUCKETS // 2
    max_exact = half // 2
    n = jnp.abs(rel)
    nf = jnp.maximum(n, 1).astype(jnp.float32)
    large = max_exact + (jnp.log(nf / max_exact) / math.log(T5_MAX_DIST / max_exact)
                         * (half - max_exact)).astype(jnp.int32)
    large = jnp.minimum(large, half - 1)
    return jnp.where(rel > 0, half, 0) + jnp.where(n < max_exact, n, large)


def dsa_mixer(h, w_in, q_g, k_g, t5_table):
    B, S, _ = h.shape
    topk = min(TOPK_MAX, S // 4)
    proj = h @ w_in
    q, k, v, qi, ki, wi = jnp.split(proj, list(A_SPLITS), axis=-1)
    q = rmsnorm(q.reshape(B, S, KV_GROUPS, Q_PER_KV, HEAD_DIM), q_g)
    k = rmsnorm(k.reshape(B, S, KV_GROUPS, HEAD_DIM), k_g)
    v = v.reshape(B, S, KV_GROUPS, HEAD_DIM)
    qi = qi.reshape(B, S, IDX_HEADS, IDX_DIM)
    ki = rms_unit(ki)
    wi = wi * (IDX_HEADS ** -0.5 * IDX_DIM ** -0.5)
    key_chunk = jnp.arange(S) // CHUNK
    scale = HEAD_DIM ** -0.5

    def block(start):
        q_b = lax.dynamic_slice_in_dim(q, start, QBLK, axis=1)
        qi_b = lax.dynamic_slice_in_dim(qi, start, QBLK, axis=1)
        wi_b = lax.dynamic_slice_in_dim(wi, start, QBLK, axis=1)
        tpos = start + jnp.arange(QBLK)
        idx_logits = jnp.einsum('bqhd,bsd->bqhs', qi_b, ki)
        score = jnp.einsum('bqhs,bqh->bqs', jax.nn.relu(idx_logits), wi_b).astype(jnp.float32)
        admissible = key_chunk[None, :] <= (tpos // CHUNK)[:, None]
        score = jnp.where(admissible[None], score, -jnp.inf)
        sel_score, sel = lax.top_k(score, topk)
        valid = jnp.isfinite(sel_score)
        k_sel = jax.vmap(lambda kb, ib: kb[ib])(k, sel)
        v_sel = jax.vmap(lambda vb, ib: vb[ib])(v, sel)
        logits = jnp.einsum('bqgrd,bqkgd->bqgrk', q_b, k_sel).astype(jnp.float32) * scale
        bias = t5_table[t5_bucket(sel - tpos[None, :, None])]
        bias = bias.reshape(B, QBLK, topk, KV_GROUPS, Q_PER_KV).transpose(0, 1, 3, 4, 2)
        logits = logits + bias.astype(jnp.float32)
        logits = jnp.where(valid[:, :, None, None, :], logits, -jnp.inf)
        p = jax.nn.softmax(logits, axis=-1).astype(v.dtype)
        o = jnp.einsum('bqgrk,bqkgd->bqgrd', p, v_sel)
        return o.reshape(B, QBLK, ATTN_DIM)

    starts = jnp.arange(S // QBLK) * QBLK
    out = lax.map(block, starts)
    return out.transpose(1, 0, 2, 3).reshape(B, S, ATTN_DIM)


def chunk_band_mixer(h, w_in, q_g, k_g, rel_table):
    B, S, _ = h.shape
    proj = h @ w_in
    q, k, v = jnp.split(proj, 3, axis=-1)
    q = rmsnorm(q.reshape(B, S, N_HEADS, HEAD_DIM), q_g)
    k = rmsnorm(k.reshape(B, S, N_HEADS, HEAD_DIM), k_g)
    v = v.reshape(B, S, N_HEADS, HEAD_DIM)
    pad = LEFT_CHUNKS * CHUNK
    k_pad = jnp.pad(k, ((0, 0), (pad, 0), (0, 0), (0, 0)))
    v_pad = jnp.pad(v, ((0, 0), (pad, 0), (0, 0), (0, 0)))
    qi = jnp.arange(CHUNK)
    kj = jnp.arange(BAND)
    dist = qi[:, None] - kj[None, :] + pad
    bias = rel_table[jnp.clip(dist, -REL_CLIP, REL_CLIP) + REL_CLIP]
    bias = bias.transpose(2, 0, 1).astype(jnp.float32)
    scale = HEAD_DIM ** -0.5

    def chunk(c):
        start = c * CHUNK
        q_c = lax.dynamic_slice_in_dim(q, start, CHUNK, axis=1)
        k_c = lax.dynamic_slice_in_dim(k_pad, start, BAND, axis=1)
        v_c = lax.dynamic_slice_in_dim(v_pad, start, BAND, axis=1)
        logits = jnp.einsum('bqhd,bkhd->bhqk', q_c, k_c).astype(jnp.float32) * scale + bias
        valid = kj >= pad - start
        logits = jnp.where(valid, logits, -jnp.inf)
        p = jax.nn.softmax(logits, axis=-1).astype(v.dtype)
        o = jnp.einsum('bhqk,bkhd->bqhd', p, v_c)
        return o.reshape(B, CHUNK, ATTN_DIM)

    out = lax.map(chunk, jnp.arange(S // CHUNK))
    return out.transpose(1, 0, 2, 3).reshape(B, S, ATTN_DIM)


def _shift(u, n):
    return u if n == 0 else jnp.pad(u[:, :-n], ((0, 0), (n, 0), (0, 0)))


def conv_ffn(h, w_up, conv_w, conv_b, w_down):
    u = h @ w_up
    y = conv_b
    for j in range(CONV_WIDTH):
        y = y + conv_w[j] * _shift(u, CONV_WIDTH - 1 - j)
    a, g = jnp.split(y, 2, axis=-1)
    return (jax.nn.silu(g) * a) @ w_down


def setup_inputs(seed: int = 0) -> dict:
    key = jax.random.key(seed)
    ks = jax.random.split(key, 14)
    nrm = jax.random.normal
    f32 = jnp.float32
    x = nrm(ks[0], (BATCH, SEQ, D_MODEL), f32)
    attn_norm_g = 1.0 + 0.02 * nrm(ks[1], (DEPTH, D_MODEL), f32)
    w_in_a = nrm(ks[2], (N_A_LAYERS, D_MODEL, A_IN), f32) * D_MODEL ** -0.5
    w_in_b = nrm(ks[3], (N_B_LAYERS, D_MODEL, B_IN), f32) * D_MODEL ** -0.5
    q_norm_g = 1.0 + 0.02 * nrm(ks[4], (DEPTH, HEAD_DIM), f32)
    k_norm_g = 1.0 + 0.02 * nrm(ks[5], (DEPTH, HEAD_DIM), f32)
    t5_bias = 0.5 * nrm(ks[6], (T5_BUCKETS, N_HEADS), f32)
    rel_bias_b = 0.5 * nrm(ks[7], (N_B_LAYERS, 2 * REL_CLIP + 1, N_HEADS), f32)
    w_out = nrm(ks[8], (DEPTH, ATTN_DIM, D_MODEL), f32) * ATTN_DIM ** -0.5
    ffn_norm_g = 1.0 + 0.02 * nrm(ks[9], (DEPTH, D_MODEL), f32)
    w_up = nrm(ks[10], (DEPTH, D_MODEL, 2 * D_FF), f32) * D_MODEL ** -0.5
    conv_w = 0.3 * nrm(ks[11], (DEPTH, CONV_WIDTH, 2 * D_FF), f32) \
        + jnp.zeros((CONV_WIDTH,), f32).at[CONV_WIDTH - 1].set(1.0)[None, :, None]
    conv_b = 0.02 * nrm(ks[12], (DEPTH, 2 * D_FF), f32)
    w_down = nrm(ks[13], (DEPTH, D_FF, D_MODEL), f32) * D_FF ** -0.5
    return {"x": x, "attn_norm_g": attn_norm_g, "w_in_a": w_in_a, "w_in_b": w_in_b,
            "q_norm_g": q_norm_g, "k_norm_g": k_norm_g, "t5_bias": t5_bias,
            "rel_bias_b": rel_bias_b, "w_out": w_out, "ffn_norm_g": ffn_norm_g,
            "w_up": w_up, "conv_w": conv_w, "conv_b": conv_b, "w_down": w_down}


def reference(x, attn_norm_g, w_in_a, w_in_b, q_norm_g, k_norm_g, t5_bias,
              rel_bias_b, w_out, ffn_norm_g, w_up, conv_w, conv_b, w_down):
    h = x
    for i in range(DEPTH):
        n = rmsnorm(h, attn_norm_g[i])
        if i % N_MIXERS == 0:
            m = dsa_mixer(n, w_in_a[i // N_MIXERS], q_norm_g[i], k_norm_g[i], t5_bias)
        else:
            m = chunk_band_mixer(n, w_in_b[i // N_MIXERS], q_norm_g[i], k_norm_g[i],
                                 rel_bias_b[i // N_MIXERS])
        h = h + m @ w_out[i]
        h = h + conv_ffn(rmsnorm(h, ffn_norm_g[i]), w_up[i], conv_w[i], conv_b[i], w_down[i])
    return h
```

```python
import functools

import numpy as np
import jax
import jax.numpy as jnp
from jax import lax
from jax.experimental import pallas as pl
from jax.experimental.pallas import tpu as pltpu

F32 = jnp.float32
BF16 = jnp.bfloat16
I32 = jnp.int32

EPS = 1e-6
CHUNK = 64
HEAD_DIM = 64
KV_GROUPS = 4
Q_PER_KV = 4
IDX_HEADS = 8
IDX_DIM = 64
TOPK_MAX = 256
T5_BUCKETS = 32
LEFT_CHUNKS = 8
REL_CLIP = 256

LANE = 128
V7X_VMEM_BYTES = 64 * 2**20
VMEM_LIMIT = 56 * 2**20

NEG = -1e30
INT_MIN = -2**31

QA = 256
PAD_A = 256
TA = 512
SEG = 1024
QB = 128
SB = 1024
PAD_B = LEFT_CHUNKS * CHUNK
WIN_B = PAD_B + QB


def _cparams(sem):
    return pltpu.CompilerParams(dimension_semantics=sem, vmem_limit_bytes=VMEM_LIMIT)


def _const_spec(shape):
    nd = len(shape)
    return pl.BlockSpec(shape, lambda *_: (0,) * nd, pipeline_mode=pl.Buffered(1))


def _lo_half(shape):
    return (lax.broadcasted_iota(I32, shape, len(shape) - 1) & HEAD_DIM) == 0


def _dot_t(a, b):
    return lax.dot_general(a, b, (((1,), (1,)), ((), ())), preferred_element_type=F32)


def _dot(a, b):
    return jnp.dot(a, b, preferred_element_type=F32)


def _head_rms(y):
    lo = _lo_half(y.shape)
    z = y * y
    sa = jnp.sum(jnp.where(lo, z, 0.0), axis=-1, keepdims=True)
    sb = jnp.sum(jnp.where(lo, 0.0, z), axis=-1, keepdims=True)
    inv = jnp.where(lo, lax.rsqrt(sa * (1.0 / HEAD_DIM) + EPS), lax.rsqrt(sb * (1.0 / HEAD_DIM) + EPS))
    return y * inv


def _proj_kernel(x_ref, g_ref, *rest, segs, cw):
    ns = len(segs)
    w_refs, s_refs, o_refs = rest[:ns], rest[ns:2 * ns], rest[2 * ns:3 * ns]
    n_scr = rest[3 * ns]
    x = x_ref[...]
    ms = jnp.mean(x * x, axis=-1, keepdims=True)
    n_scr[...] = (x * lax.rsqrt(ms + EPS) * g_ref[...]).astype(n_scr.dtype)
    for (cols, headnorm), w_ref, s_ref, o_ref in zip(segs, w_refs, s_refs, o_refs):
        for c0 in range(0, cols, cw):
            c1 = min(c0 + cw, cols)
            y = _dot(n_scr[...], w_ref[:, c0:c1])
            for l0 in range(0, c1 - c0, LANE):
                yl = y[:, l0:l0 + LANE]
                if headnorm:
                    yl = _head_rms(yl)
                o_ref[:, c0 + l0:c0 + l0 + LANE] = (yl * s_ref[:, c0 + l0:c0 + l0 + LANE]).astype(o_ref.dtype)


def _in_proj(x, g, segs, tm=512, cw=256):
    n, d = x.shape
    ws = [s[0].astype(BF16) for s in segs]
    ss = [s[1].astype(F32).reshape(1, -1) for s in segs]
    meta = tuple((int(s[0].shape[1]), bool(s[2])) for s in segs)
    in_specs = [pl.BlockSpec((tm, d), lambda i: (i, 0)), _const_spec((1, d))]
    in_specs += [_const_spec(w.shape) for w in ws] + [_const_spec(s.shape) for s in ss]
    out_specs = [pl.BlockSpec((tm, c), lambda i: (i, 0)) for c, _ in meta]
    out_shape = [jax.ShapeDtypeStruct((n, c), s[3]) for (c, _), s in zip(meta, segs)]
    return pl.pallas_call(
        functools.partial(_proj_kernel, segs=meta, cw=cw),
        grid=(n // tm,),
        in_specs=in_specs,
        out_specs=out_specs,
        out_shape=out_shape,
        scratch_shapes=[pltpu.VMEM((tm, d), BF16)],
        compiler_params=_cparams(("arbitrary",)),
        name="in_proj",
    )(x, g.astype(F32).reshape(1, d), *ws, *ss)


def _toeplitz_kernel(vec_ref, off_ref, o_ref, *, rows, cols, band):
    n = vec_ref.shape[-1]
    x = jnp.broadcast_to(vec_ref[0], (rows, n))
    t = pltpu.roll(x, 0, 1, stride=1, stride_axis=0)[:, :cols] - off_ref[0][:, :1]
    if band is not None:
        r = lax.broadcasted_iota(I32, (rows, cols), 0)
        c = lax.broadcasted_iota(I32, (rows, cols), 1)
        lo = (r // CHUNK) * CHUNK
        t = jnp.where((c >= lo) & (c < lo + band), t, NEG)
    o_ref[0] = t


def _toeplitz(vec, off, rows, cols, band=None):
    h, n = vec.shape
    return pl.pallas_call(
        functools.partial(_toeplitz_kernel, rows=rows, cols=cols, band=band),
        grid=(h,),
        in_specs=[pl.BlockSpec((1, 1, n), lambda i: (i, 0, 0)), pl.BlockSpec((1, 1, LANE), lambda i: (i, 0, 0))],
        out_specs=pl.BlockSpec((1, rows, cols), lambda i: (i, 0, 0)),
        out_shape=jax.ShapeDtypeStruct((h, rows, cols), F32),
        compiler_params=_cparams(("arbitrary",)),
        name="toeplitz_bias",
    )(vec.reshape(h, 1, n), jnp.broadcast_to(off.reshape(h, 1, 1), (h, 1, LANE)))


def _t5_bucket_static(rel):
    n = np.abs(rel)
    large = 8 + sum((n >= t).astype(np.int64) for t in (12, 16, 23, 32, 46, 64, 91))
    return np.where(rel > 0, 16, 0) + np.where(n < 8, n, large)


def _bias_a(t5_table):
    n = 2 * TA
    j = np.arange(n)
    d = np.where(j < n // 2, j, j - n)
    bucket = _t5_bucket_static(d - PAD_A)
    vec = t5_table[bucket, :].T.astype(F32)
    far = t5_table[T5_BUCKETS // 2 - 1, :].astype(F32)
    return _toeplitz(vec, far, QA, TA)


def _bias_b(rel_table):
    n = 1024
    j = np.arange(n)
    d = np.where(j < WIN_B, j, j - n)
    idx = np.clip(PAD_B - d, -REL_CLIP, REL_CLIP) + REL_CLIP
    vec = rel_table[idx, :].T.astype(F32)
    return _toeplitz(vec, jnp.zeros((vec.shape[0],), F32), QB, WIN_B, band=PAD_B + CHUNK)


def _attn_b_kernel(q_ref, k_ref, v_ref, b_ref, o_ref):
    j = pl.program_id(2)
    lo = _lo_half((QB, LANE))
    bias = b_ref[...].reshape(2 * QB, WIN_B)
    col = lax.broadcasted_iota(I32, (2 * QB, WIN_B), 1)

    def body(t, carry):
        q0 = pl.multiple_of(j * SB + t * QB, QB)
        qs = q_ref[0, pl.ds(pl.multiple_of(t * QB, QB), QB), :]
        zero = jnp.zeros_like(qs)
        qq = jnp.concatenate([jnp.where(lo, qs, zero), jnp.where(lo, zero, qs)], axis=0)
        s = _dot_t(qq, k_ref[0, pl.ds(q0, WIN_B), :]) + bias
        s = jnp.where(col >= PAD_B - q0, s, NEG)
        m = jnp.max(s, axis=-1, keepdims=True)
        p = jnp.exp(s - m)
        l = jnp.sum(p, axis=-1, keepdims=True)
        pv = _dot(p.astype(v_ref.dtype), v_ref[0, pl.ds(q0, WIN_B), :]) / l
        o_ref[0, pl.ds(pl.multiple_of(t * QB, QB), QB), :] = jnp.where(lo, pv[:QB], pv[QB:]).astype(o_ref.dtype)
        return carry

    lax.fori_loop(0, SB // QB, body, 0)


def _attn_b(q, kp, vp, bias):
    b, s, d = q.shape
    npair = d // LANE
    return pl.pallas_call(
        _attn_b_kernel,
        grid=(b, npair, s // SB),
        in_specs=[
            pl.BlockSpec((1, SB, LANE), lambda bi, p, j: (bi, j, p)),
            pl.BlockSpec((1, s + PAD_B, LANE), lambda bi, p, j: (bi, 0, p)),
            pl.BlockSpec((1, s + PAD_B, LANE), lambda bi, p, j: (bi, 0, p)),
            pl.BlockSpec((2, QB, WIN_B), lambda bi, p, j: (p, 0, 0)),
        ],
        out_specs=pl.BlockSpec((1, SB, LANE), lambda bi, p, j: (bi, j, p)),
        out_shape=jax.ShapeDtypeStruct((b, s, d), q.dtype),
        compiler_params=_cparams(("arbitrary", "arbitrary", "arbitrary")),
        name="attn_band",
    )(q, kp, vp, bias)


RG = 64
F1 = 1 << 10
F2 = 1 << 20


def _row_totals(field):
    ones = jnp.ones((LANE, LANE), BF16)
    return _dot(field.astype(F32).astype(BF16), ones)


def _attn_a_kernel(q_ref, qi_ref, wi_ref, k_ref, v_ref, ki_ref, b_ref, o_ref,
                   keys, x_scr, s_scr, p_scr, qs_scr, qis_scr, wb_scr, acc_scr, m_scr, l_scr, al_scr,
                   thr_scr, cnt_scr):
    i = pl.program_id(1)
    lo = _lo_half((QA, LANE))
    nrg = QA // RG
    upt = TA // LANE

    for t in range(8):
        qt = q_ref[0, :, t * LANE:(t + 1) * LANE]
        gp, r = divmod(t, 4)
        zero = jnp.zeros_like(qt)
        qs_scr[pl.ds(((2 * gp) * 4 + r) * QA, QA), :] = jnp.where(lo, qt, zero)
        qs_scr[pl.ds(((2 * gp + 1) * 4 + r) * QA, QA), :] = jnp.where(lo, zero, qt)
    for p in range(IDX_HEADS // 2):
        qt = qi_ref[0, :, p * LANE:(p + 1) * LANE]
        zero = jnp.zeros_like(qt)
        qis_scr[pl.ds((2 * p) * QA, QA), :] = jnp.where(lo, qt, zero)
        qis_scr[pl.ds((2 * p + 1) * QA, QA), :] = jnp.where(lo, zero, qt)
    for h in range(IDX_HEADS):
        wb_scr[h] = jnp.broadcast_to(wi_ref[0, :, h:h + 1], (QA, LANE))

    seg0 = ((i + 1) // 4) * (SEG // LANE)
    keys[pl.ds(seg0, SEG // LANE)] = jnp.full((SEG // LANE, QA, LANE), INT_MIN, I32)

    def score_tile(tt, carry):
        kt = ki_ref[0, pl.ds(pl.multiple_of(tt * 256, 256), 256), :]
        x_scr[...] = _dot_t(qis_scr[...], kt)
        for hb in range(2):
            rows = pl.ds(hb * 128, 128)
            sc = jnp.zeros((128, 256), F32)
            for h in range(IDX_HEADS):
                w = wb_scr[h, rows, :]
                sc = sc + jnp.concatenate([w, w], axis=1) * jnp.maximum(x_scr[pl.ds(h * QA + hb * 128, 128), :], 0.0)
            bits = lax.bitcast_convert_type(sc, I32)
            key = bits ^ ((bits >> 31) & 0x7FFFFFFF)
            col = lax.broadcasted_iota(I32, (128, 256), 1)
            row = lax.broadcasted_iota(I32, (128, 256), 0) + hb * 128
            adm = (col < (row // CHUNK + 1) * CHUNK) | (tt <= i)
            key = jnp.where(adm, key, INT_MIN)
            keys[2 * tt, rows, :] = key[:, :LANE]
            keys[2 * tt + 1, rows, :] = key[:, LANE:]
        return carry

    lax.fori_loop(1, i + 2, score_tile, 0)

    nseg = (i + 5) // 4
    thr_scr[...] = jnp.full((QA, LANE), INT_MIN, I32)

    def scan(rg, code_fn):
        rows = pl.ds(rg * RG, RG)

        def seg_body(sg, acc):
            for cc in range(SEG // LANE):
                acc = acc + code_fn(keys[sg * (SEG // LANE) + cc, rows, :], sg * SEG + cc * LANE)
            return acc

        cnt_scr[rows, :] = lax.fori_loop(0, nseg, seg_body, jnp.zeros((RG, LANE), I32))

    def round_body(r, carry):
        step = jnp.left_shift(jnp.int32(1), 30 - 2 * r)
        for rg in range(nrg):
            c1 = thr_scr[pl.ds(rg * RG, RG), :] + step
            c2 = c1 + step
            c3 = c2 + step
            scan(rg, lambda kk, _: jnp.where(kk >= c3, 1 + F1 + F2,
                                             jnp.where(kk >= c2, 1 + F1, jnp.where(kk >= c1, 1, 0))))
        acc = cnt_scr[...]
        n1 = _row_totals(acc & (F1 - 1))
        n2 = _row_totals((acc >> 10) & (F1 - 1))
        n3 = _row_totals(acc >> 20)
        k = float(TOPK_MAX)
        inc = (n1 >= k).astype(I32) + (n2 >= k).astype(I32) + (n3 >= k).astype(I32)
        thr_scr[...] = thr_scr[...] + inc * step
        return carry

    lax.fori_loop(0, 16, round_body, 0)

    for rg in range(nrg):
        th = thr_scr[pl.ds(rg * RG, RG), :]
        scan(rg, lambda kk, _: jnp.where(kk > th, 1 + F1, jnp.where(kk >= th, 1, 0)))
    acc = cnt_scr[...]
    n_ge = _row_totals(acc & (F1 - 1))
    n_gt = _row_totals(acc >> 10)
    tie = (n_ge > float(TOPK_MAX)) & (thr_scr[...] > INT_MIN)

    @pl.when(jnp.max(jnp.where(tie, 1.0, 0.0)) > 0.0)
    def _():
        need = float(TOPK_MAX) - n_gt
        al_scr[0] = need
        al_scr[1] = jnp.where(tie, 1.0, 0.0)
        m_scr[0] = jnp.zeros((QA, LANE), F32)

        def bit_body(bi, carry):
            step = jnp.left_shift(jnp.int32(1), 13 - bi)
            for rg in range(nrg):
                rows = pl.ds(rg * RG, RG)
                th = thr_scr[rows, :]
                xc = m_scr[0, rows, :].astype(I32) + step
                lane = lax.broadcasted_iota(I32, (RG, LANE), 1)
                scan(rg, lambda kk, c0: jnp.where((kk == th) & (lane + c0 < xc), 1, 0))
            f = _row_totals(cnt_scr[...])
            m_scr[0] = jnp.where(f < al_scr[0], m_scr[0] + step.astype(F32), m_scr[0])
            return carry

        lax.fori_loop(0, 14, bit_body, 0)
        for rg in range(nrg):
            rows = pl.ds(rg * RG, RG)
            th = thr_scr[rows, :]
            jcut = m_scr[0, rows, :].astype(I32) + 1
            tr = al_scr[1, rows, :] > 0.0
            lane = lax.broadcasted_iota(I32, (RG, LANE), 1)

            def demote(u, carry):
                kk = keys[u, rows, :]
                keys[u, rows, :] = jnp.where(tr & (kk == th) & (lane + u * LANE >= jcut), INT_MIN, kk)
                return carry

            lax.fori_loop(0, nseg * (SEG // LANE), demote, 0)

    thr_scr[...] = jnp.maximum(thr_scr[...], INT_MIN + 1)

    m_scr[...] = jnp.full(m_scr.shape, NEG, F32)
    l_scr[...] = jnp.zeros(l_scr.shape, F32)
    acc_scr[...] = jnp.zeros(acc_scr.shape, F32)

    def attend(start, with_bias):
        u0 = start // LANE
        for g in range(KV_GROUPS):
            gp, half = divmod(g, 2)
            buf = g % 2
            kg = k_ref[0, pl.ds(start, TA), gp * LANE:(gp + 1) * LANE]
            s_scr[buf] = _dot_t(qs_scr[pl.ds(g * 4 * QA, 4 * QA), :], kg)
            for r in range(Q_PER_KV):
                hd = g * Q_PER_KV + r
                for rb in range(nrg):
                    rows = pl.ds(rb * RG, RG)
                    th = thr_scr[rows, :]
                    s = s_scr[buf, pl.ds(r * QA + rb * RG, RG), :]
                    if with_bias:
                        s = s + b_ref[hd, rows, :]
                    sel = jnp.concatenate([keys[u0 + u, rows, :] >= th for u in range(upt)], axis=1)
                    s = jnp.where(sel, s, NEG)
                    mt = s[:, :LANE]
                    for u in range(1, upt):
                        mt = jnp.maximum(mt, s[:, u * LANE:(u + 1) * LANE])
                    m_old = m_scr[hd, rows, :]
                    m_new = jnp.maximum(m_old, jnp.max(mt, axis=-1, keepdims=True))
                    alpha = jnp.exp(m_old - m_new)
                    p = jnp.exp(s - jnp.concatenate([m_new] * upt, axis=1))
                    ps = p[:, :LANE]
                    for u in range(1, upt):
                        ps = ps + p[:, u * LANE:(u + 1) * LANE]
                    l_scr[hd, rows, :] = alpha * l_scr[hd, rows, :] + ps
                    m_scr[hd, rows, :] = m_new
                    al_scr[r, rows, :] = alpha
                    p_scr[buf, pl.ds(r * QA + rb * RG, RG), :] = p.astype(p_scr.dtype)
            pv = _dot(p_scr[buf], v_ref[0, pl.ds(start, TA), gp * LANE:(gp + 1) * LANE])
            for r in range(Q_PER_KV):
                hd = g * Q_PER_KV + r
                acc_scr[hd] = al_scr[r] * acc_scr[hd] + pv[r * QA:(r + 1) * QA, :]

    def far_body(t, carry):
        attend(pl.multiple_of((i & 1) * 256 + t * TA, 256), False)
        return carry

    lax.fori_loop(0, i // 2, far_body, 0)
    attend(pl.multiple_of(i * QA, QA), True)

    for t in range(8):
        gp, r = divmod(t, 4)
        ha = (2 * gp) * 4 + r
        hb = (2 * gp + 1) * 4 + r
        oa = acc_scr[ha] / jnp.sum(l_scr[ha], axis=-1, keepdims=True)
        ob = acc_scr[hb] / jnp.sum(l_scr[hb], axis=-1, keepdims=True)
        o_ref[0, :, t * LANE:(t + 1) * LANE] = jnp.where(lo, oa, ob).astype(o_ref.dtype)


def _attn_a(q, qi, wi, kp, vp, kip, bias):
    b, s, d = q.shape
    sp = s + PAD_A
    units = -(-(sp) // SEG) * (SEG // LANE)
    nh = d // HEAD_DIM
    return pl.pallas_call(
        _attn_a_kernel,
        grid=(b, s // QA),
        in_specs=[
            pl.BlockSpec((1, QA, d), lambda bi, i: (bi, i, 0)),
            pl.BlockSpec((1, QA, qi.shape[-1]), lambda bi, i: (bi, i, 0)),
            pl.BlockSpec((1, QA, LANE), lambda bi, i: (bi, i, 0)),
            pl.BlockSpec((1, sp, kp.shape[-1]), lambda bi, i: (bi, 0, 0), pipeline_mode=pl.Buffered(1)),
            pl.BlockSpec((1, sp, vp.shape[-1]), lambda bi, i: (bi, 0, 0), pipeline_mode=pl.Buffered(1)),
            pl.BlockSpec((1, sp, LANE), lambda bi, i: (bi, 0, 0), pipeline_mode=pl.Buffered(1)),
            _const_spec(bias.shape),
        ],
        out_specs=pl.BlockSpec((1, QA, d), lambda bi, i: (bi, i, 0)),
        out_shape=jax.ShapeDtypeStruct((b, s, d), q.dtype),
        scratch_shapes=[
            pltpu.VMEM((units, QA, LANE), I32),
            pltpu.VMEM((IDX_HEADS * QA, 256), F32),
            pltpu.VMEM((2, Q_PER_KV * QA, TA), F32),
            pltpu.VMEM((2, Q_PER_KV * QA, TA), q.dtype),
            pltpu.VMEM((nh * QA, LANE), q.dtype),
            pltpu.VMEM((IDX_HEADS * QA, LANE), qi.dtype),
            pltpu.VMEM((IDX_HEADS, QA, LANE), F32),
            pltpu.VMEM((nh, QA, LANE), F32),
            pltpu.VMEM((nh, QA, LANE), F32),
            pltpu.VMEM((nh, QA, LANE), F32),
            pltpu.VMEM((Q_PER_KV, QA, LANE), F32),
            pltpu.VMEM((QA, LANE), I32),
            pltpu.VMEM((QA, LANE), I32),
        ],
        compiler_params=_cparams(("arbitrary", "arbitrary")),
        name="attn_sparse",
    )(q, qi, wi, kp, vp, kip, bias)


HALO = 16


def _ffn_kernel(h_ref, hh_ref, m_ref, mh_ref, wo_ref, g_ref, wup_ref, cw_ref, cb_ref, wdn_ref, o_ref,
                me_scr, n_scr, h1_scr, u_scr, acc_scr, *, tm, dff, cw, seq):
    i = pl.program_id(0)
    first = (i * tm) % seq == 0
    me_scr[0:HALO] = mh_ref[...]
    me_scr[HALO:] = m_ref[...]
    h1_scr[...] = _dot(me_scr[...], wo_ref[...])
    h1_scr[0:HALO] = h1_scr[0:HALO] + hh_ref[...]
    h1_scr[HALO:] = h1_scr[HALO:] + h_ref[...]
    h1 = h1_scr[...]
    ms = jnp.mean(h1 * h1, axis=-1, keepdims=True)
    n = h1 * lax.rsqrt(ms + EPS) * g_ref[...]
    row = lax.broadcasted_iota(I32, n.shape, 0)
    n_scr[...] = jnp.where((row < HALO) & first, 0.0, n).astype(n_scr.dtype)
    acc_scr[...] = jnp.zeros(acc_scr.shape, F32)
    for c in range(dff // cw):
        ys = []
        for part in range(2):
            c0 = part * dff + c * cw
            u_scr[part] = _dot(n_scr[...], wup_ref[:, c0:c0 + cw])
            y = cb_ref[:, c0:c0 + cw] + cw_ref[0:1, c0:c0 + cw] * u_scr[part, HALO - 2:HALO - 2 + tm, :]
            y = y + cw_ref[1:2, c0:c0 + cw] * u_scr[part, HALO - 1:HALO - 1 + tm, :]
            y = y + cw_ref[2:3, c0:c0 + cw] * u_scr[part, HALO:HALO + tm, :]
            ys.append(y)
        a, gte = ys
        act = (gte * (1.0 / (1.0 + jnp.exp(-gte)))) * a
        acc_scr[...] += _dot(act.astype(n_scr.dtype), wdn_ref[c * cw:(c + 1) * cw, :])
    o_ref[...] = h1_scr[HALO:] + acc_scr[...]


def _out_ffn(h, m, w_out, g, w_up, conv_w, conv_b, w_down, seq, tm=512, cw=256):
    n, d = h.shape
    dff = w_down.shape[0]
    hb = tm // HALO
    halo = lambda i: (jnp.maximum(i * hb - 1, 0), 0)
    return pl.pallas_call(
        functools.partial(_ffn_kernel, tm=tm, dff=dff, cw=cw, seq=seq),
        grid=(n // tm,),
        in_specs=[
            pl.BlockSpec((tm, d), lambda i: (i, 0)),
            pl.BlockSpec((HALO, d), halo),
            pl.BlockSpec((tm, d), lambda i: (i, 0)),
            pl.BlockSpec((HALO, d), halo),
            _const_spec((d, d)),
            _const_spec((1, d)),
            _const_spec((d, 2 * dff)),
            _const_spec((3, 2 * dff)),
            _const_spec((1, 2 * dff)),
            _const_spec((dff, d)),
        ],
        out_specs=pl.BlockSpec((tm, d), lambda i: (i, 0)),
        out_shape=jax.ShapeDtypeStruct((n, d), F32),
        scratch_shapes=[
            pltpu.VMEM((tm + HALO, d), BF16),
            pltpu.VMEM((tm + HALO, d), BF16),
            pltpu.VMEM((tm + HALO, d), F32),
            pltpu.VMEM((2, tm + HALO, cw), F32),
            pltpu.VMEM((tm, d), F32),
        ],
        compiler_params=_cparams(("arbitrary",)),
        name="out_ffn",
    )(h, h, m, m, w_out.astype(BF16), g.astype(F32).reshape(1, d), w_up.astype(BF16),
      conv_w.astype(F32), conv_b.astype(F32).reshape(1, -1), w_down.astype(BF16))


_HEAD_PERM = np.array([8 * gp + 4 * half + r for gp in range(2) for r in range(4) for half in range(2)])
_COL_PERM = (_HEAD_PERM[:, None] * HEAD_DIM + np.arange(HEAD_DIM)[None, :]).reshape(-1)


def _pad_front(x, b, s, pad):
    return jnp.pad(x.reshape(b, s, x.shape[-1]), ((0, 0), (pad, 0), (0, 0)))


def _mixer_a(h2, b, s, g_attn, w_in, q_g, k_g, bias):
    ad = Q_PER_KV * KV_GROUPS * HEAD_DIM
    kd = KV_GROUPS * HEAD_DIM
    o_q, o_k, o_v, o_qi = 0, ad, ad + kd, ad + 2 * kd
    o_ki = o_qi + IDX_HEADS * IDX_DIM
    o_wi = o_ki + IDX_DIM
    w_ki = w_in[:, o_ki:o_wi]
    w_wi = jnp.pad(w_in[:, o_wi:o_wi + IDX_HEADS], ((0, 0), (0, LANE - IDX_HEADS)))
    wi_scale = IDX_HEADS ** -0.5 * IDX_DIM ** -0.5
    ones = lambda c: jnp.ones((c,), F32)
    segs = [
        (w_in[:, o_q:o_k][:, _COL_PERM], jnp.tile(q_g, ad // HEAD_DIM) * HEAD_DIM ** -0.5, True, BF16),
        (w_in[:, o_k:o_v], jnp.tile(k_g, KV_GROUPS), True, BF16),
        (w_in[:, o_v:o_qi], ones(kd), False, BF16),
        (w_in[:, o_qi:o_ki], ones(IDX_HEADS * IDX_DIM), False, BF16),
        (jnp.concatenate([w_ki, w_ki], axis=1), ones(LANE), True, BF16),
        (w_wi, ones(LANE) * wi_scale, False, F32),
    ]
    q, k, v, qi, ki, wi = _in_proj(h2, g_attn, segs)
    r3 = lambda x: x.reshape(b, s, x.shape[-1])
    o = _attn_a(r3(q), r3(qi), r3(wi), _pad_front(k, b, s, PAD_A), _pad_front(v, b, s, PAD_A),
                _pad_front(ki, b, s, PAD_A), bias)
    return o.reshape(b * s, ad)


def _mixer_b(h2, b, s, g_attn, w_in, q_g, k_g, bias):
    d = w_in.shape[1] // 3
    nh = d // HEAD_DIM
    segs = [
        (w_in[:, :d], jnp.tile(q_g, nh) * HEAD_DIM ** -0.5, True, BF16),
        (w_in[:, d:2 * d], jnp.tile(k_g, nh), True, BF16),
        (w_in[:, 2 * d:], jnp.ones((d,), F32), False, BF16),
    ]
    q, k, v = _in_proj(h2, g_attn, segs)
    o = _attn_b(q.reshape(b, s, d), _pad_front(k, b, s, PAD_B), _pad_front(v, b, s, PAD_B), bias)
    return o.reshape(b * s, d)


def kernel(x, attn_norm_g, w_in_a, w_in_b, q_norm_g, k_norm_g, t5_bias, rel_bias_b, w_out, ffn_norm_g, w_up, conv_w, conv_b, w_down):
    b, s, d = x.shape
    depth = attn_norm_g.shape[0]
    assert s % SB == 0 and s % QA == 0 and min(TOPK_MAX, s // 4) == TOPK_MAX
    h = x.reshape(b * s, d)
    bias_a = _bias_a(t5_bias)
    for i in range(depth):
        if i % 2 == 0:
            m = _mixer_a(h, b, s, attn_norm_g[i], w_in_a[i // 2], q_norm_g[i], k_norm_g[i], bias_a)
            wo = w_out[i][_COL_PERM, :]
        else:
            m = _mixer_b(h, b, s, attn_norm_g[i], w_in_b[i // 2], q_norm_g[i], k_norm_g[i], _bias_b(rel_bias_b[i // 2]))
            wo = w_out[i]
        h = _out_ffn(h, m, wo, ffn_norm_g[i], w_up[i], conv_w[i], conv_b[i], w_down[i], s)
    return h.reshape(b, s, d)
```

```python
import functools

import numpy as np
import jax
import jax.numpy as jnp
from jax import lax
from jax.experimental import pallas as pl
from jax.experimental.pallas import tpu as pltpu

F32 = jnp.float32
BF16 = jnp.bfloat16
I32 = jnp.int32

EPS = 1e-6
CHUNK = 64
HEAD_DIM = 64
KV_GROUPS = 4
Q_PER_KV = 4
IDX_HEADS = 8
IDX_DIM = 64
TOPK_MAX = 256
T5_BUCKETS = 32
LEFT_CHUNKS = 8
REL_CLIP = 256

LANE = 128
V7X_VMEM_BYTES = 64 * 2**20
VMEM_LIMIT = 56 * 2**20

NEG = -1e30
LOG2E = 1.4426950408889634
INT_MIN = -2**31

QA = 256
PAD_A = 256
TA = 512
SEG = 1024
QB = 128
SB = 1024
PAD_B = LEFT_CHUNKS * CHUNK
WIN_B = PAD_B + QB
UNROLL_B = 2


def _cparams(sem):
    return pltpu.CompilerParams(dimension_semantics=sem, vmem_limit_bytes=VMEM_LIMIT)


def _const_spec(shape):
    nd = len(shape)
    return pl.BlockSpec(shape, lambda *_: (0,) * nd, pipeline_mode=pl.Buffered(1))


def _lo_half(shape):
    return (lax.broadcasted_iota(I32, shape, len(shape) - 1) & HEAD_DIM) == 0


def _dot_t(a, b):
    return lax.dot_general(a, b, (((1,), (1,)), ((), ())), preferred_element_type=F32)


def _dot(a, b):
    return jnp.dot(a, b, preferred_element_type=F32)


def _head_rms(y):
    lo = _lo_half(y.shape)
    z = y * y
    sa = jnp.sum(jnp.where(lo, z, 0.0), axis=-1, keepdims=True)
    sb = jnp.sum(jnp.where(lo, 0.0, z), axis=-1, keepdims=True)
    inv = jnp.where(lo, lax.rsqrt(sa * (1.0 / HEAD_DIM) + EPS), lax.rsqrt(sb * (1.0 / HEAD_DIM) + EPS))
    return y * inv


def _proj_kernel(x_ref, g_ref, *rest, segs, cw):
    ns = len(segs)
    w_refs, s_refs, o_refs = rest[:ns], rest[ns:2 * ns], rest[2 * ns:3 * ns]
    n_scr = rest[3 * ns]
    x = x_ref[...]
    ms = jnp.mean(x * x, axis=-1, keepdims=True)
    n_scr[...] = (x * lax.rsqrt(ms + EPS) * g_ref[...]).astype(n_scr.dtype)
    for (cols, headnorm), w_ref, s_ref, o_ref in zip(segs, w_refs, s_refs, o_refs):
        for c0 in range(0, cols, cw):
            c1 = min(c0 + cw, cols)
            y = _dot(n_scr[...], w_ref[:, c0:c1])
            for l0 in range(0, c1 - c0, LANE):
                yl = y[:, l0:l0 + LANE]
                if headnorm:
                    yl = _head_rms(yl)
                o_ref[:, c0 + l0:c0 + l0 + LANE] = (yl * s_ref[:, c0 + l0:c0 + l0 + LANE]).astype(o_ref.dtype)


def _in_proj(x, g, segs, tm=512, cw=256):
    n, d = x.shape
    ws = [s[0].astype(BF16) for s in segs]
    ss = [s[1].astype(F32).reshape(1, -1) for s in segs]
    meta = tuple((int(s[0].shape[1]), bool(s[2])) for s in segs)
    in_specs = [pl.BlockSpec((tm, d), lambda i: (i, 0)), _const_spec((1, d))]
    in_specs += [_const_spec(w.shape) for w in ws] + [_const_spec(s.shape) for s in ss]
    out_specs = [pl.BlockSpec((tm, c), lambda i: (i, 0)) for c, _ in meta]
    out_shape = [jax.ShapeDtypeStruct((n, c), s[3]) for (c, _), s in zip(meta, segs)]
    return pl.pallas_call(
        functools.partial(_proj_kernel, segs=meta, cw=cw),
        grid=(n // tm,),
        in_specs=in_specs,
        out_specs=out_specs,
        out_shape=out_shape,
        scratch_shapes=[pltpu.VMEM((tm, d), BF16)],
        compiler_params=_cparams(("arbitrary",)),
        name="in_proj",
    )(x, g.astype(F32).reshape(1, d), *ws, *ss)


def _toeplitz_kernel(vec_ref, off_ref, o_ref, *, rows, cols, band):
    n = vec_ref.shape[-1]
    x = jnp.broadcast_to(vec_ref[0], (rows, n))
    t = (pltpu.roll(x, 0, 1, stride=1, stride_axis=0)[:, :cols] - off_ref[0][:, :1]) * LOG2E
    if band is not None:
        r = lax.broadcasted_iota(I32, (rows, cols), 0)
        c = lax.broadcasted_iota(I32, (rows, cols), 1)
        lo = (r // CHUNK) * CHUNK
        t = jnp.where((c >= lo) & (c < lo + band), t, NEG)
    o_ref[0] = t


def _toeplitz(vec, off, rows, cols, band=None):
    h, n = vec.shape
    return pl.pallas_call(
        functools.partial(_toeplitz_kernel, rows=rows, cols=cols, band=band),
        grid=(h,),
        in_specs=[pl.BlockSpec((1, 1, n), lambda i: (i, 0, 0)), pl.BlockSpec((1, 1, LANE), lambda i: (i, 0, 0))],
        out_specs=pl.BlockSpec((1, rows, cols), lambda i: (i, 0, 0)),
        out_shape=jax.ShapeDtypeStruct((h, rows, cols), F32),
        compiler_params=_cparams(("arbitrary",)),
        name="toeplitz_bias",
    )(vec.reshape(h, 1, n), jnp.broadcast_to(off.reshape(h, 1, 1), (h, 1, LANE)))


def _t5_bucket_static(rel):
    n = np.abs(rel)
    large = 8 + sum((n >= t).astype(np.int64) for t in (12, 16, 23, 32, 46, 64, 91))
    return np.where(rel > 0, 16, 0) + np.where(n < 8, n, large)


def _bias_a(t5_table):
    n = 2 * TA
    j = np.arange(n)
    d = np.where(j < n // 2, j, j - n)
    bucket = _t5_bucket_static(d - PAD_A)
    vec = t5_table[bucket, :].T.astype(F32)
    far = t5_table[T5_BUCKETS // 2 - 1, :].astype(F32)
    return _toeplitz(vec, far, QA, TA)


def _bias_b(rel_table):
    n = 1024
    j = np.arange(n)
    d = np.where(j < WIN_B, j, j - n)
    idx = np.clip(PAD_B - d, -REL_CLIP, REL_CLIP) + REL_CLIP
    vec = rel_table[idx, :].T.astype(F32)
    return _toeplitz(vec, jnp.zeros((vec.shape[0],), F32), QB, WIN_B, band=PAD_B + CHUNK)


def _attn_b_kernel(q_ref, k_ref, v_ref, b_ref, o_ref):
    j = pl.program_id(2)
    lo = _lo_half((QB, LANE))
    bias = b_ref[...].reshape(2 * QB, WIN_B)
    col = lax.broadcasted_iota(I32, (2 * QB, WIN_B), 1)

    def block(t):
        q0 = pl.multiple_of(j * SB + t * QB, QB)
        qs = q_ref[0, pl.ds(pl.multiple_of(t * QB, QB), QB), :]
        zero = jnp.zeros_like(qs)
        qq = jnp.concatenate([jnp.where(lo, qs, zero), jnp.where(lo, zero, qs)], axis=0)
        s = _dot_t(qq, k_ref[0, pl.ds(q0, WIN_B), :]) + bias
        s = jnp.where(col >= PAD_B - q0, s, NEG)
        m = jnp.max(s, axis=-1, keepdims=True)
        p = jnp.exp2(s - m)
        l = jnp.sum(p, axis=-1, keepdims=True)
        pv = _dot(p.astype(v_ref.dtype), v_ref[0, pl.ds(q0, WIN_B), :]) / l
        o_ref[0, pl.ds(pl.multiple_of(t * QB, QB), QB), :] = jnp.where(lo, pv[:QB], pv[QB:]).astype(o_ref.dtype)

    def body(t2, carry):
        for k in range(UNROLL_B):
            block(t2 * UNROLL_B + k)
        return carry

    lax.fori_loop(0, SB // QB // UNROLL_B, body, 0)


def _attn_b(q, kp, vp, bias):
    b, s, d = q.shape
    npair = d // LANE
    return pl.pallas_call(
        _attn_b_kernel,
        grid=(b, npair, s // SB),
        in_specs=[
            pl.BlockSpec((1, SB, LANE), lambda bi, p, j: (bi, j, p)),
            pl.BlockSpec((1, s + PAD_B, LANE), lambda bi, p, j: (bi, 0, p)),
            pl.BlockSpec((1, s + PAD_B, LANE), lambda bi, p, j: (bi, 0, p)),
            pl.BlockSpec((2, QB, WIN_B), lambda bi, p, j: (p, 0, 0)),
        ],
        out_specs=pl.BlockSpec((1, SB, LANE), lambda bi, p, j: (bi, j, p)),
        out_shape=jax.ShapeDtypeStruct((b, s, d), q.dtype),
        compiler_params=_cparams(("arbitrary", "arbitrary", "arbitrary")),
        name="attn_band",
    )(q, kp, vp, bias)


RG = 64
F1 = 1 << 10
F2 = 1 << 20


def _row_totals(field):
    ones = jnp.ones((LANE, LANE), BF16)
    return _dot(field.astype(F32).astype(BF16), ones)


def _attn_a_kernel(q_ref, qi_ref, wi_ref, k_ref, v_ref, ki_ref, b_ref, o_ref,
                   keys, x_scr, s_scr, p_scr, qs_scr, qis_scr, wb_scr, acc_scr, m_scr, mb_scr, al_scr,
                   thr_scr, cnt_scr):
    i = pl.program_id(1)
    lo = _lo_half((QA, LANE))
    nrg = QA // RG
    upt = TA // LANE

    for t in range(8):
        qt = q_ref[0, :, t * LANE:(t + 1) * LANE]
        gp, r = divmod(t, 4)
        zero = jnp.zeros_like(qt)
        qs_scr[pl.ds(((2 * gp) * 4 + r) * QA, QA), :] = jnp.where(lo, qt, zero)
        qs_scr[pl.ds(((2 * gp + 1) * 4 + r) * QA, QA), :] = jnp.where(lo, zero, qt)
    for p in range(IDX_HEADS // 2):
        qt = qi_ref[0, :, p * LANE:(p + 1) * LANE]
        zero = jnp.zeros_like(qt)
        qis_scr[pl.ds((2 * p) * QA, QA), :] = jnp.where(lo, qt, zero)
        qis_scr[pl.ds((2 * p + 1) * QA, QA), :] = jnp.where(lo, zero, qt)
    for h in range(IDX_HEADS):
        wb_scr[h] = jnp.broadcast_to(wi_ref[0, :, h:h + 1], (QA, LANE))

    seg0 = ((i + 1) // 4) * (SEG // LANE)
    keys[pl.ds(seg0, SEG // LANE)] = jnp.full((SEG // LANE, QA, LANE), INT_MIN, I32)

    def score_tile(tt, carry):
        kt = ki_ref[0, pl.ds(pl.multiple_of(tt * 256, 256), 256), :]
        x_scr[...] = _dot_t(qis_scr[...], kt)
        for hb in range(2):
            rows = pl.ds(hb * 128, 128)
            sc = jnp.zeros((128, 256), F32)
            for h in range(IDX_HEADS):
                w = wb_scr[h, rows, :]
                sc = sc + jnp.concatenate([w, w], axis=1) * jnp.maximum(x_scr[pl.ds(h * QA + hb * 128, 128), :], 0.0)
            bits = lax.bitcast_convert_type(sc, I32)
            key = bits ^ ((bits >> 31) & 0x7FFFFFFF)
            col = lax.broadcasted_iota(I32, (128, 256), 1)
            row = lax.broadcasted_iota(I32, (128, 256), 0) + hb * 128
            adm = (col < (row // CHUNK + 1) * CHUNK) | (tt <= i)
            key = jnp.where(adm, key, INT_MIN)
            keys[2 * tt, rows, :] = key[:, :LANE]
            keys[2 * tt + 1, rows, :] = key[:, LANE:]
        return carry

    lax.fori_loop(1, i + 2, score_tile, 0)

    nseg = (i + 5) // 4
    thr_scr[...] = jnp.full((QA, LANE), INT_MIN, I32)

    def scan(rg, code_fn):
        rows = pl.ds(rg * RG, RG)

        def seg_body(sg, acc):
            for cc in range(SEG // LANE):
                acc = acc + code_fn(keys[sg * (SEG // LANE) + cc, rows, :], sg * SEG + cc * LANE)
            return acc

        cnt_scr[rows, :] = lax.fori_loop(0, nseg, seg_body, jnp.zeros((RG, LANE), I32))

    def round_body(r, carry):
        step = jnp.left_shift(jnp.int32(1), 30 - 2 * r)
        for rg in range(nrg):
            c1 = thr_scr[pl.ds(rg * RG, RG), :] + step
            c2 = c1 + step
            c3 = c2 + step
            scan(rg, lambda kk, _: jnp.where(kk >= c3, 1 + F1 + F2,
                                             jnp.where(kk >= c2, 1 + F1, jnp.where(kk >= c1, 1, 0))))
        acc = cnt_scr[...]
        n1 = _row_totals(acc & (F1 - 1))
        n2 = _row_totals((acc >> 10) & (F1 - 1))
        n3 = _row_totals(acc >> 20)
        k = float(TOPK_MAX)
        inc = (n1 >= k).astype(I32) + (n2 >= k).astype(I32) + (n3 >= k).astype(I32)
        thr_scr[...] = thr_scr[...] + inc * step
        return carry

    lax.fori_loop(0, 16, round_body, 0)

    for rg in range(nrg):
        th = thr_scr[pl.ds(rg * RG, RG), :]
        scan(rg, lambda kk, _: jnp.where(kk > th, 1 + F1, jnp.where(kk >= th, 1, 0)))
    acc = cnt_scr[...]
    n_ge = _row_totals(acc & (F1 - 1))
    n_gt = _row_totals(acc >> 10)
    tie = (n_ge > float(TOPK_MAX)) & (thr_scr[...] > INT_MIN)

    @pl.when(jnp.max(jnp.where(tie, 1.0, 0.0)) > 0.0)
    def _():
        need = float(TOPK_MAX) - n_gt
        al_scr[0] = need
        al_scr[1] = jnp.where(tie, 1.0, 0.0)
        m_scr[0] = jnp.zeros((QA, LANE), F32)

        def bit_body(bi, carry):
            step = jnp.left_shift(jnp.int32(1), 13 - bi)
            for rg in range(nrg):
                rows = pl.ds(rg * RG, RG)
                th = thr_scr[rows, :]
                xc = m_scr[0, rows, :].astype(I32) + step
                lane = lax.broadcasted_iota(I32, (RG, LANE), 1)
                scan(rg, lambda kk, c0: jnp.where((kk == th) & (lane + c0 < xc), 1, 0))
            f = _row_totals(cnt_scr[...])
            m_scr[0] = jnp.where(f < al_scr[0], m_scr[0] + step.astype(F32), m_scr[0])
            return carry

        lax.fori_loop(0, 14, bit_body, 0)
        for rg in range(nrg):
            rows = pl.ds(rg * RG, RG)
            th = thr_scr[rows, :]
            jcut = m_scr[0, rows, :].astype(I32) + 1
            tr = al_scr[1, rows, :] > 0.0
            lane = lax.broadcasted_iota(I32, (RG, LANE), 1)

            def demote(u, carry):
                kk = keys[u, rows, :]
                keys[u, rows, :] = jnp.where(tr & (kk == th) & (lane + u * LANE >= jcut), INT_MIN, kk)
                return carry

            lax.fori_loop(0, nseg * (SEG // LANE), demote, 0)

    thr_scr[...] = jnp.maximum(thr_scr[...], INT_MIN + 1)

    m_scr[...] = jnp.full(m_scr.shape, NEG, F32)
    acc_scr[...] = jnp.zeros(acc_scr.shape, F32)

    def attend(start, with_bias):
        u0 = start // LANE
        for rb in range(nrg):
            rows = pl.ds(rb * RG, RG)
            th = thr_scr[rows, :]
            for u in range(upt):
                mb_scr[rows, u * LANE:(u + 1) * LANE] = jnp.where(keys[u0 + u, rows, :] >= th, 0.0, NEG)
        for g in range(KV_GROUPS):
            gp = g // 2
            buf = g % 2
            kg = k_ref[0, pl.ds(start, TA), gp * LANE:(gp + 1) * LANE]
            s_scr[buf] = _dot_t(qs_scr[pl.ds(g * 4 * QA, 4 * QA), :], kg)
            for r in range(Q_PER_KV):
                hd = g * Q_PER_KV + r
                for rb in range(nrg):
                    rows = pl.ds(rb * RG, RG)
                    s = s_scr[buf, pl.ds(r * QA + rb * RG, RG), :] + mb_scr[rows, :]
                    if with_bias:
                        s = s + b_ref[hd, rows, :]
                    mt = s[:, :LANE]
                    for u in range(1, upt):
                        mt = jnp.maximum(mt, s[:, u * LANE:(u + 1) * LANE])
                    m_old = m_scr[hd, rows, :]
                    m_new = jnp.maximum(m_old, jnp.max(mt, axis=-1, keepdims=True))
                    al_scr[r, rows, :] = jnp.exp2(m_old - m_new)
                    m_scr[hd, rows, :] = m_new
                    p = jnp.exp2(s - jnp.concatenate([m_new] * upt, axis=1))
                    p_scr[buf, pl.ds(r * QA + rb * RG, RG), :] = p.astype(p_scr.dtype)
            pv = _dot(p_scr[buf], v_ref[0, pl.ds(start, TA), g * LANE:(g + 1) * LANE])
            for r in range(Q_PER_KV):
                hd = g * Q_PER_KV + r
                acc_scr[hd] = al_scr[r] * acc_scr[hd] + pv[r * QA:(r + 1) * QA, :]

    def far_body(t, carry):
        attend(pl.multiple_of((i & 1) * 256 + t * TA, 256), False)
        return carry

    lax.fori_loop(0, i // 2, far_body, 0)
    attend(pl.multiple_of(i * QA, QA), True)

    for t in range(8):
        gp, r = divmod(t, 4)
        ha = (2 * gp) * 4 + r
        hb = (2 * gp + 1) * 4 + r
        aa = acc_scr[ha]
        ab = acc_scr[hb]
        oa = aa / pltpu.roll(aa, HEAD_DIM, 1)
        ob = ab / pltpu.roll(ab, HEAD_DIM, 1)
        o_ref[0, :, t * LANE:(t + 1) * LANE] = jnp.where(lo, oa, ob).astype(o_ref.dtype)


def _attn_a(q, qi, wi, kp, vp, kip, bias):
    b, s, d = q.shape
    sp = s + PAD_A
    units = -(-(sp) // SEG) * (SEG // LANE)
    nh = d // HEAD_DIM
    return pl.pallas_call(
        _attn_a_kernel,
        grid=(b, s // QA),
        in_specs=[
            pl.BlockSpec((1, QA, d), lambda bi, i: (bi, i, 0)),
            pl.BlockSpec((1, QA, qi.shape[-1]), lambda bi, i: (bi, i, 0)),
            pl.BlockSpec((1, QA, LANE), lambda bi, i: (bi, i, 0)),
            pl.BlockSpec((1, sp, kp.shape[-1]), lambda bi, i: (bi, 0, 0), pipeline_mode=pl.Buffered(1)),
            pl.BlockSpec((1, sp, vp.shape[-1]), lambda bi, i: (bi, 0, 0), pipeline_mode=pl.Buffered(1)),
            pl.BlockSpec((1, sp, LANE), lambda bi, i: (bi, 0, 0), pipeline_mode=pl.Buffered(1)),
            _const_spec(bias.shape),
        ],
        out_specs=pl.BlockSpec((1, QA, d), lambda bi, i: (bi, i, 0)),
        out_shape=jax.ShapeDtypeStruct((b, s, d), q.dtype),
        scratch_shapes=[
            pltpu.VMEM((units, QA, LANE), I32),
            pltpu.VMEM((IDX_HEADS * QA, 256), F32),
            pltpu.VMEM((2, Q_PER_KV * QA, TA), F32),
            pltpu.VMEM((2, Q_PER_KV * QA, TA), q.dtype),
            pltpu.VMEM((nh * QA, LANE), q.dtype),
            pltpu.VMEM((IDX_HEADS * QA, LANE), qi.dtype),
            pltpu.VMEM((IDX_HEADS, QA, LANE), F32),
            pltpu.VMEM((nh, QA, LANE), F32),
            pltpu.VMEM((nh, QA, LANE), F32),
            pltpu.VMEM((QA, TA), F32),
            pltpu.VMEM((Q_PER_KV, QA, LANE), F32),
            pltpu.VMEM((QA, LANE), I32),
            pltpu.VMEM((QA, LANE), I32),
        ],
        compiler_params=_cparams(("arbitrary", "arbitrary")),
        name="attn_sparse",
    )(q, qi, wi, kp, vp, kip, bias)


HALO = 16


def _ffn_kernel(h_ref, hh_ref, m_ref, mh_ref, wo_ref, g_ref, wup_ref, cw_ref, cb_ref, wdn_ref, o_ref,
                me_scr, n_scr, h1_scr, u_scr, acc_scr, *, tm, dff, cw, seq):
    i = pl.program_id(0)
    first = (i * tm) % seq == 0
    me_scr[0:HALO] = mh_ref[...]
    me_scr[HALO:] = m_ref[...]
    h1_scr[...] = _dot(me_scr[...], wo_ref[...])
    h1_scr[0:HALO] = h1_scr[0:HALO] + hh_ref[...]
    h1_scr[HALO:] = h1_scr[HALO:] + h_ref[...]
    h1 = h1_scr[...]
    ms = jnp.mean(h1 * h1, axis=-1, keepdims=True)
    n = h1 * lax.rsqrt(ms + EPS) * g_ref[...]
    row = lax.broadcasted_iota(I32, n.shape, 0)
    n_scr[...] = jnp.where((row < HALO) & first, 0.0, n).astype(n_scr.dtype)
    acc_scr[...] = jnp.zeros(acc_scr.shape, F32)
    for c in range(dff // cw):
        ys = []
        for part in range(2):
            c0 = part * dff + c * cw
            u_scr[part] = _dot(n_scr[...], wup_ref[:, c0:c0 + cw])
            y = cb_ref[:, c0:c0 + cw] + cw_ref[0:1, c0:c0 + cw] * u_scr[part, HALO - 2:HALO - 2 + tm, :]
            y = y + cw_ref[1:2, c0:c0 + cw] * u_scr[part, HALO - 1:HALO - 1 + tm, :]
            y = y + cw_ref[2:3, c0:c0 + cw] * u_scr[part, HALO:HALO + tm, :]
            ys.append(y)
        a, gte = ys
        act = (gte * (1.0 / (1.0 + jnp.exp(-gte)))) * a
        acc_scr[...] += _dot(act.astype(n_scr.dtype), wdn_ref[c * cw:(c + 1) * cw, :])
    o_ref[...] = h1_scr[HALO:] + acc_scr[...]


def _out_ffn(h, m, w_out, g, w_up, conv_w, conv_b, w_down, seq, tm=512, cw=256):
    n, d = h.shape
    dff = w_down.shape[0]
    hb = tm // HALO
    halo = lambda i: (jnp.maximum(i * hb - 1, 0), 0)
    return pl.pallas_call(
        functools.partial(_ffn_kernel, tm=tm, dff=dff, cw=cw, seq=seq),
        grid=(n // tm,),
        in_specs=[
            pl.BlockSpec((tm, d), lambda i: (i, 0)),
            pl.BlockSpec((HALO, d), halo),
            pl.BlockSpec((tm, d), lambda i: (i, 0)),
            pl.BlockSpec((HALO, d), halo),
            _const_spec((d, d)),
            _const_spec((1, d)),
            _const_spec((d, 2 * dff)),
            _const_spec((3, 2 * dff)),
            _const_spec((1, 2 * dff)),
            _const_spec((dff, d)),
        ],
        out_specs=pl.BlockSpec((tm, d), lambda i: (i, 0)),
        out_shape=jax.ShapeDtypeStruct((n, d), F32),
        scratch_shapes=[
            pltpu.VMEM((tm + HALO, d), BF16),
            pltpu.VMEM((tm + HALO, d), BF16),
            pltpu.VMEM((tm + HALO, d), F32),
            pltpu.VMEM((2, tm + HALO, cw), F32),
            pltpu.VMEM((tm, d), F32),
        ],
        compiler_params=_cparams(("arbitrary",)),
        name="out_ffn",
    )(h, h, m, m, w_out.astype(BF16), g.astype(F32).reshape(1, d), w_up.astype(BF16),
      conv_w.astype(F32), conv_b.astype(F32).reshape(1, -1), w_down.astype(BF16))


_HEAD_PERM = np.array([8 * gp + 4 * half + r for gp in range(2) for r in range(4) for half in range(2)])
_COL_PERM = (_HEAD_PERM[:, None] * HEAD_DIM + np.arange(HEAD_DIM)[None, :]).reshape(-1)


def _pad_front(x, b, s, pad):
    return jnp.pad(x.reshape(b, s, x.shape[-1]), ((0, 0), (pad, 0), (0, 0)))


def _mixer_a(h2, b, s, g_attn, w_in, q_g, k_g, bias):
    ad = Q_PER_KV * KV_GROUPS * HEAD_DIM
    kd = KV_GROUPS * HEAD_DIM
    o_q, o_k, o_v, o_qi = 0, ad, ad + kd, ad + 2 * kd
    o_ki = o_qi + IDX_HEADS * IDX_DIM
    o_wi = o_ki + IDX_DIM
    w_ki = w_in[:, o_ki:o_wi]
    w_wi = jnp.pad(w_in[:, o_wi:o_wi + IDX_HEADS], ((0, 0), (0, LANE - IDX_HEADS)))
    wi_scale = IDX_HEADS ** -0.5 * IDX_DIM ** -0.5
    ones = lambda c: jnp.ones((c,), F32)
    segs = [
        (w_in[:, o_q:o_k][:, _COL_PERM], jnp.tile(q_g, ad // HEAD_DIM) * (HEAD_DIM ** -0.5 * LOG2E), True, BF16),
        (w_in[:, o_k:o_v], jnp.tile(k_g, KV_GROUPS), True, BF16),
        (w_in[:, o_v:o_qi], ones(kd), False, BF16),
        (w_in[:, o_qi:o_ki], ones(IDX_HEADS * IDX_DIM), False, BF16),
        (jnp.concatenate([w_ki, w_ki], axis=1), ones(LANE), True, BF16),
        (w_wi, ones(LANE) * wi_scale, False, F32),
    ]
    q, k, v, qi, ki, wi = _in_proj(h2, g_attn, segs)
    r3 = lambda x: x.reshape(b, s, x.shape[-1])
    vp = _pad_front(v, b, s, PAD_A)
    one = jnp.ones(vp.shape[:2] + (HEAD_DIM,), vp.dtype)
    blocks = []
    for g in range(KV_GROUPS):
        vg = vp[..., g * HEAD_DIM:(g + 1) * HEAD_DIM]
        blocks += [vg, one] if g % 2 == 0 else [one, vg]
    o = _attn_a(r3(q), r3(qi), r3(wi), _pad_front(k, b, s, PAD_A), jnp.concatenate(blocks, axis=-1),
                _pad_front(ki, b, s, PAD_A), bias)
    return o.reshape(b * s, ad)


def _mixer_b(h2, b, s, g_attn, w_in, q_g, k_g, bias):
    d = w_in.shape[1] // 3
    nh = d // HEAD_DIM
    segs = [
        (w_in[:, :d], jnp.tile(q_g, nh) * (HEAD_DIM ** -0.5 * LOG2E), True, BF16),
        (w_in[:, d:2 * d], jnp.tile(k_g, nh), True, BF16),
        (w_in[:, 2 * d:], jnp.ones((d,), F32), False, BF16),
    ]
    q, k, v = _in_proj(h2, g_attn, segs)
    o = _attn_b(q.reshape(b, s, d), _pad_front(k, b, s, PAD_B), _pad_front(v, b, s, PAD_B), bias)
    return o.reshape(b * s, d)


def kernel(x, attn_norm_g, w_in_a, w_in_b, q_norm_g, k_norm_g, t5_bias, rel_bias_b, w_out, ffn_norm_g, w_up, conv_w, conv_b, w_down):
    b, s, d = x.shape
    depth = attn_norm_g.shape[0]
    assert s % SB == 0 and s % QA == 0 and min(TOPK_MAX, s // 4) == TOPK_MAX
    h = x.reshape(b * s, d)
    bias_a = _bias_a(t5_bias)
    for i in range(depth):
        if i % 2 == 0:
            m = _mixer_a(h, b, s, attn_norm_g[i], w_in_a[i // 2], q_norm_g[i], k_norm_g[i], bias_a)
            wo = w_out[i][_COL_PERM, :]
        else:
            m = _mixer_b(h, b, s, attn_norm_g[i], w_in_b[i // 2], q_norm_g[i], k_norm_g[i], _bias_b(rel_bias_b[i // 2]))
            wo = w_out[i]
        h = _out_ffn(h, m, wo, ffn_norm_g[i], w_up[i], conv_w[i], conv_b[i], w_down[i], s)
    return h.reshape(b, s, d)
```

```python
import functools

import numpy as np
import jax
import jax.numpy as jnp
from jax import lax
from jax.experimental import pallas as pl
from jax.experimental.pallas import tpu as pltpu

F32 = jnp.float32
BF16 = jnp.bfloat16
I32 = jnp.int32

EPS = 1e-6
CHUNK = 64
HEAD_DIM = 64
KV_GROUPS = 4
Q_PER_KV = 4
IDX_HEADS = 8
IDX_DIM = 64
TOPK_MAX = 256
T5_BUCKETS = 32
LEFT_CHUNKS = 8
REL_CLIP = 256

LANE = 128
SUBLANE = 8
V7X_VMEM_BYTES = 64 * 2**20
VMEM_LIMIT = 56 * 2**20

NEG = -1e30
LOG2E = 1.4426950408889634
INT_MIN = -2**31

QA = 256
PAD_A = 256
TI = 256
TA = 512
VR = HEAD_DIM + 16
CB = 64
KB = 16
BISECT_UNROLL = 2
QB = 128
SB = 1024
PAD_B = LEFT_CHUNKS * CHUNK
WIN_B = PAD_B + QB
UNROLL_B = 4


def _cparams(sem):
    return pltpu.CompilerParams(dimension_semantics=sem, vmem_limit_bytes=VMEM_LIMIT)


def _const_spec(shape):
    nd = len(shape)
    return pl.BlockSpec(shape, lambda *_: (0,) * nd, pipeline_mode=pl.Buffered(1))


def _lo_half(shape):
    return (lax.broadcasted_iota(I32, shape, len(shape) - 1) & HEAD_DIM) == 0


def _dot_t(a, b):
    return lax.dot_general(a, b, (((1,), (1,)), ((), ())), preferred_element_type=F32)


def _dot(a, b):
    return jnp.dot(a, b, preferred_element_type=F32)


def _head_rms(y):
    lo = _lo_half(y.shape)
    z = y * y
    sa = jnp.sum(jnp.where(lo, z, 0.0), axis=-1, keepdims=True)
    sb = jnp.sum(jnp.where(lo, 0.0, z), axis=-1, keepdims=True)
    inv = jnp.where(lo, lax.rsqrt(sa * (1.0 / HEAD_DIM) + EPS), lax.rsqrt(sb * (1.0 / HEAD_DIM) + EPS))
    return y * inv


def _proj_kernel(x_ref, g_ref, *rest, segs, cw):
    ns = len(segs)
    w_refs, s_refs, o_refs = rest[:ns], rest[ns:2 * ns], rest[2 * ns:3 * ns]
    n_scr = rest[3 * ns]
    x = x_ref[...]
    ms = jnp.mean(x * x, axis=-1, keepdims=True)
    n_scr[...] = (x * lax.rsqrt(ms + EPS) * g_ref[...]).astype(n_scr.dtype)
    for (cols, headnorm), w_ref, s_ref, o_ref in zip(segs, w_refs, s_refs, o_refs):
        for c0 in range(0, cols, cw):
            c1 = min(c0 + cw, cols)
            y = _dot(n_scr[...], w_ref[:, c0:c1])
            for l0 in range(0, c1 - c0, LANE):
                yl = y[:, l0:l0 + LANE]
                if headnorm:
                    yl = _head_rms(yl)
                o_ref[:, c0 + l0:c0 + l0 + LANE] = (yl * s_ref[:, c0 + l0:c0 + l0 + LANE]).astype(o_ref.dtype)


def _in_proj(x, g, segs, tm=512, cw=256):
    n, d = x.shape
    ws = [s[0].astype(BF16) for s in segs]
    ss = [s[1].astype(F32).reshape(1, -1) for s in segs]
    meta = tuple((int(s[0].shape[1]), bool(s[2])) for s in segs)
    in_specs = [pl.BlockSpec((tm, d), lambda i: (i, 0)), _const_spec((1, d))]
    in_specs += [_const_spec(w.shape) for w in ws] + [_const_spec(s.shape) for s in ss]
    out_specs = [pl.BlockSpec((tm, c), lambda i: (i, 0)) for c, _ in meta]
    out_shape = [jax.ShapeDtypeStruct((n, c), s[3]) for (c, _), s in zip(meta, segs)]
    return pl.pallas_call(
        functools.partial(_proj_kernel, segs=meta, cw=cw),
        grid=(n // tm,),
        in_specs=in_specs,
        out_specs=out_specs,
        out_shape=out_shape,
        scratch_shapes=[pltpu.VMEM((tm, d), BF16)],
        compiler_params=_cparams(("arbitrary",)),
        name="in_proj",
    )(x, g.astype(F32).reshape(1, d), *ws, *ss)


def _toeplitz_kernel(vec_ref, off_ref, o_ref, *, rows, cols, band):
    n = vec_ref.shape[-1]
    x = jnp.broadcast_to(vec_ref[0], (rows, n))
    t = (pltpu.roll(x, 0, 1, stride=1, stride_axis=0)[:, :cols] - off_ref[0][:, :1]) * LOG2E
    if band is not None:
        r = lax.broadcasted_iota(I32, (rows, cols), 0)
        c = lax.broadcasted_iota(I32, (rows, cols), 1)
        lo = (r // CHUNK) * CHUNK
        t = jnp.where((c >= lo) & (c < lo + band), t, NEG)
    o_ref[0] = t


def _toeplitz(vec, off, rows, cols, band=None):
    h, n = vec.shape
    return pl.pallas_call(
        functools.partial(_toeplitz_kernel, rows=rows, cols=cols, band=band),
        grid=(h,),
        in_specs=[pl.BlockSpec((1, 1, n), lambda i: (i, 0, 0)), pl.BlockSpec((1, 1, LANE), lambda i: (i, 0, 0))],
        out_specs=pl.BlockSpec((1, rows, cols), lambda i: (i, 0, 0)),
        out_shape=jax.ShapeDtypeStruct((h, rows, cols), F32),
        compiler_params=_cparams(("arbitrary",)),
        name="toeplitz_bias",
    )(vec.reshape(h, 1, n), jnp.broadcast_to(off.reshape(h, 1, 1), (h, 1, LANE)))


def _t5_bucket_static(rel):
    n = np.abs(rel)
    large = 8 + sum((n >= t).astype(np.int64) for t in (12, 16, 23, 32, 46, 64, 91))
    return np.where(rel > 0, 16, 0) + np.where(n < 8, n, large)


def _bias_a(t5_table):
    n = 2 * TA
    j = np.arange(n)
    d = np.where(j < n // 2, j, j - n)
    bucket = _t5_bucket_static(-d - PAD_A)
    vec = t5_table[bucket, :].T.astype(F32)
    far = t5_table[T5_BUCKETS // 2 - 1, :].astype(F32)
    return _toeplitz(vec, far, TA, QA)


def _bias_b(rel_table):
    n = 1024
    j = np.arange(n)
    d = np.where(j < WIN_B, j, j - n)
    idx = np.clip(PAD_B - d, -REL_CLIP, REL_CLIP) + REL_CLIP
    vec = rel_table[idx, :].T.astype(F32)
    return _toeplitz(vec, jnp.zeros((vec.shape[0],), F32), QB, WIN_B, band=PAD_B + CHUNK)


def _attn_b_kernel(q_ref, k_ref, v_ref, b_ref, o_ref):
    j = pl.program_id(2)
    lo = _lo_half((QB, LANE))
    bias = b_ref[...].reshape(2 * QB, WIN_B)
    col = lax.broadcasted_iota(I32, (2 * QB, WIN_B), 1)

    def block(t):
        q0 = pl.multiple_of(j * SB + t * QB, QB)
        qs = q_ref[0, pl.ds(pl.multiple_of(t * QB, QB), QB), :]
        zero = jnp.zeros_like(qs)
        qq = jnp.concatenate([jnp.where(lo, qs, zero), jnp.where(lo, zero, qs)], axis=0)
        s = _dot_t(qq, k_ref[0, pl.ds(q0, WIN_B), :]) + bias
        s = jnp.where(col >= PAD_B - q0, s, NEG)
        m = jnp.max(s, axis=-1, keepdims=True)
        p = jnp.exp2(s - m)
        l = jnp.sum(p, axis=-1, keepdims=True)
        pv = _dot(p.astype(v_ref.dtype), v_ref[0, pl.ds(q0, WIN_B), :]) / l
        o_ref[0, pl.ds(pl.multiple_of(t * QB, QB), QB), :] = jnp.where(lo, pv[:QB], pv[QB:]).astype(o_ref.dtype)

    def body(t2, carry):
        for k in range(UNROLL_B):
            block(t2 * UNROLL_B + k)
        return carry

    lax.fori_loop(0, SB // QB // UNROLL_B, body, 0)


def _attn_b(q, kp, vp, bias):
    b, s, d = q.shape
    npair = d // LANE
    return pl.pallas_call(
        _attn_b_kernel,
        grid=(b, npair, s // SB),
        in_specs=[
            pl.BlockSpec((1, SB, LANE), lambda bi, p, j: (bi, j, p)),
            pl.BlockSpec((1, s + PAD_B, LANE), lambda bi, p, j: (bi, 0, p)),
            pl.BlockSpec((1, s + PAD_B, LANE), lambda bi, p, j: (bi, 0, p)),
            pl.BlockSpec((2, QB, WIN_B), lambda bi, p, j: (p, 0, 0)),
        ],
        out_specs=pl.BlockSpec((1, SB, LANE), lambda bi, p, j: (bi, j, p)),
        out_shape=jax.ShapeDtypeStruct((b, s, d), q.dtype),
        compiler_params=_cparams(("arbitrary", "arbitrary", "arbitrary")),
        name="attn_band",
    )(q, kp, vp, bias)


def _count(keys, start, ntile, pred):
    nacc = 4

    def tile(t, accs):
        accs = list(accs)
        base = start + t * TA
        for cb in range(TA // CB):
            blk = keys[pl.ds(base + cb * CB, CB), :]
            for c in range(CB // SUBLANE):
                kk = blk[c * SUBLANE:(c + 1) * SUBLANE]
                a = accs[c % nacc]
                accs[c % nacc] = jnp.where(pred(kk, base + cb * CB + c * SUBLANE), a + 1, a)
        return tuple(accs)

    accs = lax.fori_loop(0, ntile, tile, tuple(jnp.zeros((SUBLANE, QA), I32) for _ in range(nacc)))
    tot = (accs[0] + accs[1]) + (accs[2] + accs[3])
    return jnp.sum(tot, axis=0, keepdims=True).astype(F32)


def _attn_a_kernel(qt_ref, qit_ref, wit_ref, k_ref, ki_ref, vt_ref, b_ref, o_ref,
                   keys, x_scr, s_scr, p_scr, qs_scr, qis_scr, acc_scr, m_scr, al_scr, mb_scr, gm_scr,
                   lo_scr, hi_scr, f_scr):
    i = pl.program_id(1)
    topk = float(TOPK_MAX)
    sub_lo = lax.broadcasted_iota(I32, (LANE, QA), 0) < HEAD_DIM

    for t in range(8):
        qt = qt_ref[0, t * LANE:(t + 1) * LANE, :]
        gp, r = divmod(t, 4)
        zero = jnp.zeros_like(qt)
        qs_scr[2 * gp, :, r * QA:(r + 1) * QA] = jnp.where(sub_lo, qt, zero)
        qs_scr[2 * gp + 1, :, r * QA:(r + 1) * QA] = jnp.where(sub_lo, zero, qt)
    for p in range(IDX_HEADS // 2):
        qt = qit_ref[0, p * LANE:(p + 1) * LANE, :]
        zero = jnp.zeros_like(qt)
        qis_scr[:, (2 * p) * QA:(2 * p + 1) * QA] = jnp.where(sub_lo, qt, zero)
        qis_scr[:, (2 * p + 1) * QA:(2 * p + 2) * QA] = jnp.where(sub_lo, zero, qt)

    keys[0:PAD_A, :] = jnp.full((PAD_A, QA), INT_MIN, I32)
    gm_scr[...] = jnp.full((TI, QA), INT_MIN, I32)
    qcol = lax.broadcasted_iota(I32, (CB, QA), 1)
    krow = lax.broadcasted_iota(I32, (CB, QA), 0)

    def score_tile(tt, carry):
        base = pl.multiple_of(tt * TI, TI)
        x_scr[...] = _dot(ki_ref[0, pl.ds(base, TI), :], qis_scr[...])
        for cb in range(TI // CB):
            sc = jnp.zeros((CB, QA), F32)
            for h in range(IDX_HEADS):
                sc = sc + wit_ref[0, h:h + 1, :] * jnp.maximum(x_scr[cb * CB:(cb + 1) * CB, h * QA:(h + 1) * QA], 0.0)
            bits = lax.bitcast_convert_type(sc, I32)
            key = bits ^ ((bits >> 31) & 0x7FFFFFFF)
            adm = (krow + cb * CB < (qcol // CHUNK + 1) * CHUNK) | (tt <= i)
            key = jnp.where(adm, key, INT_MIN)
            keys[pl.ds(base + cb * CB, CB), :] = key
            gm_scr[cb * CB:(cb + 1) * CB, :] = jnp.maximum(gm_scr[cb * CB:(cb + 1) * CB, :], key)
        return carry

    lax.fori_loop(1, i + 2, score_tile, 0)

    gm = gm_scr[...]
    lo_scr[...] = jnp.min(gm, axis=0, keepdims=True)
    hi_scr[...] = jnp.max(gm, axis=0, keepdims=True) + 1
    s_start = pl.multiple_of((i & 1) * PAD_A, PAD_A)
    s_ntile = (i + 2) // 2

    def open_rows():
        return jnp.max(jnp.where(hi_scr[...] - 1 > lo_scr[...], 1.0, 0.0))

    def bisect(active):
        for _ in range(BISECT_UNROLL):
            lo, hi = lo_scr[...], hi_scr[...]
            mid = (lo >> 1) + (hi >> 1) + (lo & hi & 1)
            cnt = _count(keys, s_start, s_ntile, lambda kk, _: kk >= mid)
            ge = cnt >= topk
            exact = cnt == topk
            lo_scr[...] = jnp.where(ge, mid, lo)
            hi_scr[...] = jnp.where(exact, mid + 1, jnp.where(ge, hi, mid))
        return open_rows()

    lax.while_loop(lambda a: a > 0.0, bisect, open_rows())

    thr = lo_scr[...]
    n_ge = _count(keys, s_start, s_ntile, lambda kk, _: kk >= thr)
    n_gt = _count(keys, s_start, s_ntile, lambda kk, _: kk > thr)
    tie = (n_ge > topk) & (thr > INT_MIN)

    @pl.when(jnp.max(jnp.where(tie, 1.0, 0.0)) > 0.0)
    def _():
        need = topk - n_gt
        f_scr[...] = jnp.zeros((1, QA), I32)
        rid = lax.broadcasted_iota(I32, (SUBLANE, QA), 0)

        def bit_body(bi, carry):
            xc = f_scr[...] + jnp.left_shift(jnp.int32(1), 13 - bi)
            f = _count(keys, s_start, s_ntile, lambda kk, r0: (kk == thr) & (rid + r0 < xc))
            f_scr[...] = jnp.where(f < need, xc, f_scr[...])
            return carry

        lax.fori_loop(0, 14, bit_body, 0)
        jcut = f_scr[...] + 1

        def demote(c, carry):
            r0 = s_start + c * SUBLANE
            kk = keys[pl.ds(r0, SUBLANE), :]
            keys[pl.ds(r0, SUBLANE), :] = jnp.where(tie & (kk == thr) & (rid + r0 >= jcut), INT_MIN, kk)
            return carry

        lax.fori_loop(0, s_ntile * (TA // SUBLANE), demote, 0)

    thr = jnp.maximum(thr, INT_MIN + 1)

    m_scr[...] = jnp.full(m_scr.shape, NEG, F32)
    acc_scr[...] = jnp.zeros(acc_scr.shape, F32)

    def attend(start, with_bias):
        for cb in range(TA // CB):
            rows = pl.ds(start + cb * CB, CB)
            mb_scr[cb * CB:(cb + 1) * CB, :] = jnp.where(keys[rows, :] >= thr, 0.0, NEG)
        blk = start // TI

        def qk(g):
            s_scr[g % 2] = _dot(k_ref[0, pl.ds(start, TA), (g // 2) * LANE:(g // 2 + 1) * LANE], qs_scr[g])

        qk(0)
        for g in range(KV_GROUPS):
            buf = g % 2
            if g + 1 < KV_GROUPS:
                qk(g + 1)
            hds = [g * Q_PER_KV + r for r in range(Q_PER_KV)]

            def logits(kb):
                rows = slice(kb * KB, (kb + 1) * KB)
                mb = mb_scr[rows, :]
                out = []
                for r in range(Q_PER_KV):
                    s = s_scr[buf, rows, r * QA:(r + 1) * QA] + mb
                    if with_bias:
                        s = s + b_ref[hds[r], rows, :]
                    out.append(s)
                return out

            mx = logits(0)
            for kb in range(1, TA // KB):
                mx = [jnp.maximum(a, s) for a, s in zip(mx, logits(kb))]
            m_new = []
            for r in range(Q_PER_KV):
                m_old = m_scr[hds[r]]
                mn = jnp.maximum(m_old, jnp.max(mx[r], axis=0, keepdims=True))
                al_scr[hds[r]] = jnp.exp2(m_old - mn)
                m_scr[hds[r]] = mn
                m_new.append(jnp.broadcast_to(mn, (KB, QA)))
            for kb in range(TA // KB):
                ss = logits(kb)
                for r in range(Q_PER_KV):
                    p_scr[buf, kb * KB:(kb + 1) * KB, r * QA:(r + 1) * QA] = jnp.exp2(ss[r] - m_new[r]).astype(p_scr.dtype)
            for r in range(Q_PER_KV):
                cols = slice(r * QA, (r + 1) * QA)
                pv = _dot(vt_ref[0, g, blk], p_scr[buf, 0:TI, cols]) + _dot(vt_ref[0, g, blk + 1], p_scr[buf, TI:TA, cols])
                acc_scr[hds[r]] = al_scr[hds[r]] * acc_scr[hds[r]] + pv

    def far_body(t, carry):
        attend(pl.multiple_of((i & 1) * PAD_A + t * TA, PAD_A), False)
        return carry

    lax.fori_loop(0, i // 2, far_body, 0)
    attend(pl.multiple_of(i * QA, QA), True)

    for t in range(8):
        gp, r = divmod(t, 4)
        aa = acc_scr[(2 * gp) * Q_PER_KV + r]
        ab = acc_scr[(2 * gp + 1) * Q_PER_KV + r]
        oa = aa[:HEAD_DIM] / aa[HEAD_DIM:HEAD_DIM + 1]
        ob = ab[:HEAD_DIM] / ab[HEAD_DIM:HEAD_DIM + 1]
        o_ref[0, :, t * LANE:(t + 1) * LANE] = jnp.concatenate([oa, ob], axis=0).T.astype(o_ref.dtype)


def _attn_a(qt, qit, wit, kp, kip, vt, bias):
    b, d, s = qt.shape
    sp = s + PAD_A
    nh = d // HEAD_DIM
    return pl.pallas_call(
        _attn_a_kernel,
        grid=(b, s // QA),
        in_specs=[
            pl.BlockSpec((1, d, QA), lambda bi, i: (bi, 0, i)),
            pl.BlockSpec((1, qit.shape[1], QA), lambda bi, i: (bi, 0, i)),
            pl.BlockSpec((1, IDX_HEADS, QA), lambda bi, i: (bi, 0, i)),
            pl.BlockSpec((1, sp, kp.shape[-1]), lambda bi, i: (bi, 0, 0), pipeline_mode=pl.Buffered(1)),
            pl.BlockSpec((1, sp, LANE), lambda bi, i: (bi, 0, 0), pipeline_mode=pl.Buffered(1)),
            pl.BlockSpec((1,) + vt.shape[1:], lambda bi, i: (bi, 0, 0, 0, 0), pipeline_mode=pl.Buffered(1)),
            _const_spec(bias.shape),
        ],
        out_specs=pl.BlockSpec((1, QA, d), lambda bi, i: (bi, i, 0)),
        out_shape=jax.ShapeDtypeStruct((b, s, d), qt.dtype),
        scratch_shapes=[
            pltpu.VMEM((sp, QA), I32),
            pltpu.VMEM((TI, IDX_HEADS * QA), F32),
            pltpu.VMEM((2, TA, Q_PER_KV * QA), F32),
            pltpu.VMEM((2, TA, Q_PER_KV * QA), qt.dtype),
            pltpu.VMEM((KV_GROUPS, LANE, Q_PER_KV * QA), qt.dtype),
            pltpu.VMEM((LANE, IDX_HEADS * QA), qit.dtype),
            pltpu.VMEM((nh, VR, QA), F32),
            pltpu.VMEM((nh, 1, QA), F32),
            pltpu.VMEM((nh, 1, QA), F32),
            pltpu.VMEM((TA, QA), F32),
            pltpu.VMEM((TI, QA), I32),
            pltpu.VMEM((1, QA), I32),
            pltpu.VMEM((1, QA), I32),
            pltpu.VMEM((1, QA), I32),
        ],
        compiler_params=_cparams(("arbitrary", "arbitrary")),
        name="attn_sparse",
    )(qt, qit, wit, kp, kip, vt, bias)


HALO = 16


def _ffn_kernel(h_ref, hh_ref, m_ref, mh_ref, wo_ref, g_ref, wup_ref, cw_ref, cb_ref, wdn_ref, o_ref,
                me_scr, n_scr, h1_scr, u_scr, acc_scr, *, tm, dff, cw, seq):
    i = pl.program_id(0)
    first = (i * tm) % seq == 0
    me_scr[0:HALO] = mh_ref[...]
    me_scr[HALO:] = m_ref[...]
    h1_scr[...] = _dot(me_scr[...], wo_ref[...])
    h1_scr[0:HALO] = h1_scr[0:HALO] + hh_ref[...]
    h1_scr[HALO:] = h1_scr[HALO:] + h_ref[...]
    h1 = h1_scr[...]
    ms = jnp.mean(h1 * h1, axis=-1, keepdims=True)
    n = h1 * lax.rsqrt(ms + EPS) * g_ref[...]
    row = lax.broadcasted_iota(I32, n.shape, 0)
    n_scr[...] = jnp.where((row < HALO) & first, 0.0, n).astype(n_scr.dtype)
    acc_scr[...] = jnp.zeros(acc_scr.shape, F32)

    def up(c):
        for part in range(2):
            c0 = part * dff + c * cw
            u_scr[c % 2, part] = _dot(n_scr[...], wup_ref[:, c0:c0 + cw])

    up(0)
    for c in range(dff // cw):
        if c + 1 < dff // cw:
            up(c + 1)
        ys = []
        for part in range(2):
            c0 = part * dff + c * cw
            u = u_scr.at[c % 2, part]
            y = cb_ref[:, c0:c0 + cw] + cw_ref[0:1, c0:c0 + cw] * u[HALO - 2:HALO - 2 + tm, :]
            y = y + cw_ref[1:2, c0:c0 + cw] * u[HALO - 1:HALO - 1 + tm, :]
            y = y + cw_ref[2:3, c0:c0 + cw] * u[HALO:HALO + tm, :]
            ys.append(y)
        a, gte = ys
        act = (gte * (1.0 / (1.0 + jnp.exp(-gte)))) * a
        acc_scr[...] += _dot(act.astype(n_scr.dtype), wdn_ref[c * cw:(c + 1) * cw, :])
    o_ref[...] = h1_scr[HALO:] + acc_scr[...]


def _out_ffn(h, m, w_out, g, w_up, conv_w, conv_b, w_down, seq, tm=512, cw=256):
    n, d = h.shape
    dff = w_down.shape[0]
    hb = tm // HALO
    halo = lambda i: (jnp.maximum(i * hb - 1, 0), 0)
    return pl.pallas_call(
        functools.partial(_ffn_kernel, tm=tm, dff=dff, cw=cw, seq=seq),
        grid=(n // tm,),
        in_specs=[
            pl.BlockSpec((tm, d), lambda i: (i, 0)),
            pl.BlockSpec((HALO, d), halo),
            pl.BlockSpec((tm, d), lambda i: (i, 0)),
            pl.BlockSpec((HALO, d), halo),
            _const_spec((d, d)),
            _const_spec((1, d)),
            _const_spec((d, 2 * dff)),
            _const_spec((3, 2 * dff)),
            _const_spec((1, 2 * dff)),
            _const_spec((dff, d)),
        ],
        out_specs=pl.BlockSpec((tm, d), lambda i: (i, 0)),
        out_shape=jax.ShapeDtypeStruct((n, d), F32),
        scratch_shapes=[
            pltpu.VMEM((tm + HALO, d), BF16),
            pltpu.VMEM((tm + HALO, d), BF16),
            pltpu.VMEM((tm + HALO, d), F32),
            pltpu.VMEM((2, 2, tm + HALO, cw), F32),
            pltpu.VMEM((tm, d), F32),
        ],
        compiler_params=_cparams(("arbitrary",)),
        name="out_ffn",
    )(h, h, m, m, w_out.astype(BF16), g.astype(F32).reshape(1, d), w_up.astype(BF16),
      conv_w.astype(F32), conv_b.astype(F32).reshape(1, -1), w_down.astype(BF16))


_HEAD_PERM = np.array([8 * gp + 4 * half + r for gp in range(2) for r in range(4) for half in range(2)])
_COL_PERM = (_HEAD_PERM[:, None] * HEAD_DIM + np.arange(HEAD_DIM)[None, :]).reshape(-1)


def _pad_front(x, b, s, pad):
    return jnp.pad(x.reshape(b, s, x.shape[-1]), ((0, 0), (pad, 0), (0, 0)))


def _mixer_a(h2, b, s, g_attn, w_in, q_g, k_g, bias):
    ad = Q_PER_KV * KV_GROUPS * HEAD_DIM
    kd = KV_GROUPS * HEAD_DIM
    o_q, o_k, o_v, o_qi = 0, ad, ad + kd, ad + 2 * kd
    o_ki = o_qi + IDX_HEADS * IDX_DIM
    o_wi = o_ki + IDX_DIM
    w_ki = w_in[:, o_ki:o_wi]
    w_wi = jnp.pad(w_in[:, o_wi:o_wi + IDX_HEADS], ((0, 0), (0, LANE - IDX_HEADS)))
    wi_scale = IDX_HEADS ** -0.5 * IDX_DIM ** -0.5
    ones = lambda c: jnp.ones((c,), F32)
    segs = [
        (w_in[:, o_q:o_k][:, _COL_PERM], jnp.tile(q_g, ad // HEAD_DIM) * (HEAD_DIM ** -0.5 * LOG2E), True, BF16),
        (w_in[:, o_k:o_v], jnp.tile(k_g, KV_GROUPS), True, BF16),
        (w_in[:, o_v:o_qi], ones(kd), False, BF16),
        (w_in[:, o_qi:o_ki], ones(IDX_HEADS * IDX_DIM), False, BF16),
        (jnp.concatenate([w_ki, w_ki], axis=1), ones(LANE), True, BF16),
        (w_wi, ones(LANE) * wi_scale, False, F32),
    ]
    q, k, v, qi, ki, wi = _in_proj(h2, g_attn, segs)
    tr = lambda x: x.reshape(b, s, x.shape[-1]).transpose(0, 2, 1)
    sp = s + PAD_A
    vt = jnp.pad(tr(v), ((0, 0), (0, 0), (PAD_A, 0))).reshape(b, KV_GROUPS, HEAD_DIM, sp)
    vt = jnp.concatenate([vt, jnp.ones((b, KV_GROUPS, VR - HEAD_DIM, sp), vt.dtype)], axis=2)
    vt = vt.reshape(b, KV_GROUPS, VR, sp // TI, TI).transpose(0, 1, 3, 2, 4)
    o = _attn_a(tr(q), tr(qi), tr(wi[:, :IDX_HEADS]), _pad_front(k, b, s, PAD_A), _pad_front(ki, b, s, PAD_A),
                vt, bias)
    return o.reshape(b * s, ad)


def _mixer_b(h2, b, s, g_attn, w_in, q_g, k_g, bias):
    d = w_in.shape[1] // 3
    nh = d // HEAD_DIM
    segs = [
        (w_in[:, :d], jnp.tile(q_g, nh) * (HEAD_DIM ** -0.5 * LOG2E), True, BF16),
        (w_in[:, d:2 * d], jnp.tile(k_g, nh), True, BF16),
        (w_in[:, 2 * d:], jnp.ones((d,), F32), False, BF16),
    ]
    q, k, v = _in_proj(h2, g_attn, segs)
    o = _attn_b(q.reshape(b, s, d), _pad_front(k, b, s, PAD_B), _pad_front(v, b, s, PAD_B), bias)
    return o.reshape(b * s, d)


def kernel(x, attn_norm_g, w_in_a, w_in_b, q_norm_g, k_norm_g, t5_bias, rel_bias_b, w_out, ffn_norm_g, w_up, conv_w, conv_b, w_down):
    b, s, d = x.shape
    depth = attn_norm_g.shape[0]
    assert s % SB == 0 and s % QA == 0 and min(TOPK_MAX, s // 4) == TOPK_MAX
    h = x.reshape(b * s, d)
    bias_a = _bias_a(t5_bias)
    for i in range(depth):
        if i % 2 == 0:
            m = _mixer_a(h, b, s, attn_norm_g[i], w_in_a[i // 2], q_norm_g[i], k_norm_g[i], bias_a)
            wo = w_out[i][_COL_PERM, :]
        else:
            m = _mixer_b(h, b, s, attn_norm_g[i], w_in_b[i // 2], q_norm_g[i], k_norm_g[i], _bias_b(rel_bias_b[i // 2]))
            wo = w_out[i]
        h = _out_ffn(h, m, wo, ffn_norm_g[i], w_up[i], conv_w[i], conv_b[i], w_down[i], s)
    return h.reshape(b, s, d)
```

```python
import functools

import numpy as np
import jax
import jax.numpy as jnp
from jax import lax
from jax.experimental import pallas as pl
from jax.experimental.pallas import tpu as pltpu

F32 = jnp.float32
BF16 = jnp.bfloat16
I32 = jnp.int32

EPS = 1e-6
CHUNK = 64
HEAD_DIM = 64
KV_GROUPS = 4
Q_PER_KV = 4
IDX_HEADS = 8
IDX_DIM = 64
TOPK_MAX = 256
T5_BUCKETS = 32
LEFT_CHUNKS = 8
REL_CLIP = 256

LANE = 128
SUBLANE = 8
V7X_VMEM_BYTES = 64 * 2**20
VMEM_LIMIT = 56 * 2**20

NEG = -1e30
LOG2E = 1.4426950408889634
INT_MIN = -2**31
INT_MAX = 2**31 - 1

QA = 256
PAD_A = 256
TI = 256
TA = 512
VR = HEAD_DIM + 16
CB = 64
KB = 16
BISECT_UNROLL = 2
SNAP_SPAN = 64.0
QB = 128
SB = 1024
PAD_B = LEFT_CHUNKS * CHUNK
WIN_B = PAD_B + QB
UNROLL_B = 8


def _cparams(sem):
    return pltpu.CompilerParams(dimension_semantics=sem, vmem_limit_bytes=VMEM_LIMIT)


def _const_spec(shape):
    nd = len(shape)
    return pl.BlockSpec(shape, lambda *_: (0,) * nd, pipeline_mode=pl.Buffered(1))


def _lo_half(shape):
    return (lax.broadcasted_iota(I32, shape, len(shape) - 1) & HEAD_DIM) == 0


def _dot_t(a, b):
    return lax.dot_general(a, b, (((1,), (1,)), ((), ())), preferred_element_type=F32)


def _dot(a, b):
    return jnp.dot(a, b, preferred_element_type=F32)


def _head_rms(y):
    lo = _lo_half(y.shape)
    z = y * y
    sa = jnp.sum(jnp.where(lo, z, 0.0), axis=-1, keepdims=True)
    sb = jnp.sum(jnp.where(lo, 0.0, z), axis=-1, keepdims=True)
    inv = jnp.where(lo, lax.rsqrt(sa * (1.0 / HEAD_DIM) + EPS), lax.rsqrt(sb * (1.0 / HEAD_DIM) + EPS))
    return y * inv


def _proj_kernel(x_ref, g_ref, *rest, segs, cw):
    ns = len(segs)
    w_refs, s_refs, o_refs = rest[:ns], rest[ns:2 * ns], rest[2 * ns:3 * ns]
    n_scr = rest[3 * ns]
    x = x_ref[...]
    ms = jnp.mean(x * x, axis=-1, keepdims=True)
    n_scr[...] = (x * lax.rsqrt(ms + EPS) * g_ref[...]).astype(n_scr.dtype)
    for (cols, headnorm), w_ref, s_ref, o_ref in zip(segs, w_refs, s_refs, o_refs):
        for c0 in range(0, cols, cw):
            c1 = min(c0 + cw, cols)
            y = _dot(n_scr[...], w_ref[:, c0:c1])
            for l0 in range(0, c1 - c0, LANE):
                yl = y[:, l0:l0 + LANE]
                if headnorm:
                    yl = _head_rms(yl)
                o_ref[:, c0 + l0:c0 + l0 + LANE] = (yl * s_ref[:, c0 + l0:c0 + l0 + LANE]).astype(o_ref.dtype)


def _in_proj(x, g, segs, tm=512, cw=256):
    n, d = x.shape
    ws = [s[0].astype(BF16) for s in segs]
    ss = [s[1].astype(F32).reshape(1, -1) for s in segs]
    meta = tuple((int(s[0].shape[1]), bool(s[2])) for s in segs)
    in_specs = [pl.BlockSpec((tm, d), lambda i: (i, 0)), _const_spec((1, d))]
    in_specs += [_const_spec(w.shape) for w in ws] + [_const_spec(s.shape) for s in ss]
    out_specs = [pl.BlockSpec((tm, c), lambda i: (i, 0)) for c, _ in meta]
    out_shape = [jax.ShapeDtypeStruct((n, c), s[3]) for (c, _), s in zip(meta, segs)]
    return pl.pallas_call(
        functools.partial(_proj_kernel, segs=meta, cw=cw),
        grid=(n // tm,),
        in_specs=in_specs,
        out_specs=out_specs,
        out_shape=out_shape,
        scratch_shapes=[pltpu.VMEM((tm, d), BF16)],
        compiler_params=_cparams(("arbitrary",)),
        name="in_proj",
    )(x, g.astype(F32).reshape(1, d), *ws, *ss)


def _toeplitz_kernel(vec_ref, off_ref, o_ref, *, rows, cols, band):
    n = vec_ref.shape[-1]
    x = jnp.broadcast_to(vec_ref[0], (rows, n))
    t = (pltpu.roll(x, 0, 1, stride=1, stride_axis=0)[:, :cols] - off_ref[0][:, :1]) * LOG2E
    if band is not None:
        r = lax.broadcasted_iota(I32, (rows, cols), 0)
        c = lax.broadcasted_iota(I32, (rows, cols), 1)
        lo = (r // CHUNK) * CHUNK
        t = jnp.where((c >= lo) & (c < lo + band), t, NEG)
    o_ref[0] = t


def _toeplitz(vec, off, rows, cols, band=None):
    h, n = vec.shape
    return pl.pallas_call(
        functools.partial(_toeplitz_kernel, rows=rows, cols=cols, band=band),
        grid=(h,),
        in_specs=[pl.BlockSpec((1, 1, n), lambda i: (i, 0, 0)), pl.BlockSpec((1, 1, LANE), lambda i: (i, 0, 0))],
        out_specs=pl.BlockSpec((1, rows, cols), lambda i: (i, 0, 0)),
        out_shape=jax.ShapeDtypeStruct((h, rows, cols), F32),
        compiler_params=_cparams(("arbitrary",)),
        name="toeplitz_bias",
    )(vec.reshape(h, 1, n), jnp.broadcast_to(off.reshape(h, 1, 1), (h, 1, LANE)))


def _t5_bucket_static(rel):
    n = np.abs(rel)
    large = 8 + sum((n >= t).astype(np.int64) for t in (12, 16, 23, 32, 46, 64, 91))
    return np.where(rel > 0, 16, 0) + np.where(n < 8, n, large)


def _bias_a(t5_table):
    n = 2 * TA
    j = np.arange(n)
    d = np.where(j < n // 2, j, j - n)
    bucket = _t5_bucket_static(-d - PAD_A)
    vec = t5_table[bucket, :].T.astype(F32)
    far = t5_table[T5_BUCKETS // 2 - 1, :].astype(F32)
    return _toeplitz(vec, far, TA, QA)


def _bias_b(rel_table):
    n = 1024
    j = np.arange(n)
    d = np.where(j < WIN_B, j, j - n)
    idx = np.clip(PAD_B - d, -REL_CLIP, REL_CLIP) + REL_CLIP
    vec = rel_table[idx, :].T.astype(F32)
    return _toeplitz(vec, jnp.zeros((vec.shape[0],), F32), QB, WIN_B, band=PAD_B + CHUNK)


def _attn_b_kernel(q_ref, k_ref, v_ref, b_ref, o_ref):
    j = pl.program_id(2)
    lo = _lo_half((QB, LANE))
    bias = b_ref[...].reshape(2 * QB, WIN_B)
    col = lax.broadcasted_iota(I32, (2 * QB, WIN_B), 1)

    def block(t):
        q0 = pl.multiple_of(j * SB + t * QB, QB)
        qs = q_ref[0, pl.ds(pl.multiple_of(t * QB, QB), QB), :]
        zero = jnp.zeros_like(qs)
        qq = jnp.concatenate([jnp.where(lo, qs, zero), jnp.where(lo, zero, qs)], axis=0)
        s = _dot_t(qq, k_ref[0, pl.ds(q0, WIN_B), :]) + bias
        s = jnp.where(col >= PAD_B - q0, s, NEG)
        m = jnp.max(s, axis=-1, keepdims=True)
        p = jnp.exp2(s - m)
        l = jnp.sum(p, axis=-1, keepdims=True)
        pv = _dot(p.astype(v_ref.dtype), v_ref[0, pl.ds(q0, WIN_B), :]) / l
        o_ref[0, pl.ds(pl.multiple_of(t * QB, QB), QB), :] = jnp.where(lo, pv[:QB], pv[QB:]).astype(o_ref.dtype)

    def body(t2, carry):
        for k in range(UNROLL_B):
            block(t2 * UNROLL_B + k)
        return carry

    lax.fori_loop(0, SB // QB // UNROLL_B, body, 0)


def _attn_b(q, kp, vp, bias):
    b, s, d = q.shape
    npair = d // LANE
    return pl.pallas_call(
        _attn_b_kernel,
        grid=(b, npair, s // SB),
        in_specs=[
            pl.BlockSpec((1, SB, LANE), lambda bi, p, j: (bi, j, p)),
            pl.BlockSpec((1, s + PAD_B, LANE), lambda bi, p, j: (bi, 0, p)),
            pl.BlockSpec((1, s + PAD_B, LANE), lambda bi, p, j: (bi, 0, p)),
            pl.BlockSpec((2, QB, WIN_B), lambda bi, p, j: (p, 0, 0)),
        ],
        out_specs=pl.BlockSpec((1, SB, LANE), lambda bi, p, j: (bi, j, p)),
        out_shape=jax.ShapeDtypeStruct((b, s, d), q.dtype),
        compiler_params=_cparams(("arbitrary", "arbitrary", "arbitrary")),
        name="attn_band",
    )(q, kp, vp, bias)


def _count(keys, start, ntile, pred):
    nacc = 4

    def tile(t, accs):
        accs = list(accs)
        base = start + t * TA
        for cb in range(TA // CB):
            blk = keys[pl.ds(base + cb * CB, CB), :]
            for c in range(CB // SUBLANE):
                kk = blk[c * SUBLANE:(c + 1) * SUBLANE]
                a = accs[c % nacc]
                accs[c % nacc] = jnp.where(pred(kk, base + cb * CB + c * SUBLANE), a + 1, a)
        return tuple(accs)

    accs = lax.fori_loop(0, ntile, tile, tuple(jnp.zeros((SUBLANE, QA), I32) for _ in range(nacc)))
    tot = (accs[0] + accs[1]) + (accs[2] + accs[3])
    return jnp.sum(tot, axis=0, keepdims=True).astype(F32)


def _snap(keys, start, ntile, lo, hi):
    def tile(t, accs):
        mn, mx = list(accs[:2]), list(accs[2:])
        base = start + t * TA
        for cb in range(TA // CB):
            blk = keys[pl.ds(base + cb * CB, CB), :]
            for c in range(CB // SUBLANE):
                kk = blk[c * SUBLANE:(c + 1) * SUBLANE]
                mn[c % 2] = jnp.minimum(mn[c % 2], jnp.where(kk >= lo, kk, INT_MAX))
                mx[c % 2] = jnp.maximum(mx[c % 2], jnp.where(kk < hi, kk, INT_MIN))
        return tuple(mn + mx)

    top = jnp.full((SUBLANE, QA), INT_MAX, I32)
    bot = jnp.full((SUBLANE, QA), INT_MIN, I32)
    mn0, mn1, mx0, mx1 = lax.fori_loop(0, ntile, tile, (top, top, bot, bot))
    return (jnp.min(jnp.minimum(mn0, mn1), axis=0, keepdims=True),
            jnp.max(jnp.maximum(mx0, mx1), axis=0, keepdims=True))


def _demote_ties(keys, start, ntile, thr, keep, tie):
    rid = lax.broadcasted_iota(I32, (SUBLANE, QA), 0)

    def block(c, seen):
        r0 = start + c * CB
        blk = keys[pl.ds(r0, CB), :]
        out = []
        for s in range(CB // SUBLANE):
            kk = blk[s * SUBLANE:(s + 1) * SUBLANE]
            eq = (kk == thr) & tie
            e = jnp.where(eq, 1, 0)
            for sh in (1, 2, 4):
                e = e + jnp.where(rid >= sh, pltpu.roll(e, sh, 0), 0)
            out.append(jnp.where(eq & (seen + e > keep), INT_MIN, kk))
            seen = seen + e[SUBLANE - 1:SUBLANE]
        keys[pl.ds(r0, CB), :] = jnp.concatenate(out, axis=0)
        return seen

    lax.fori_loop(0, ntile * (TA // CB), block, jnp.zeros((1, QA), I32))


def _attn_a_kernel(qt_ref, qit_ref, wit_ref, k_ref, ki_ref, vt_ref, b_ref, o_ref,
                   keys, x_scr, s_scr, p_scr, qs_scr, qis_scr, acc_scr, m_scr, al_scr, mb_scr, gm_scr,
                   lo_scr, hi_scr, cl_scr, ch_scr):
    i = pl.program_id(1)
    topk = float(TOPK_MAX)
    sub_lo = lax.broadcasted_iota(I32, (LANE, QA), 0) < HEAD_DIM

    for t in range(8):
        qt = qt_ref[0, t * LANE:(t + 1) * LANE, :]
        gp, r = divmod(t, 4)
        zero = jnp.zeros_like(qt)
        qs_scr[2 * gp, :, r * QA:(r + 1) * QA] = jnp.where(sub_lo, qt, zero)
        qs_scr[2 * gp + 1, :, r * QA:(r + 1) * QA] = jnp.where(sub_lo, zero, qt)
    for p in range(IDX_HEADS // 2):
        qt = qit_ref[0, p * LANE:(p + 1) * LANE, :]
        zero = jnp.zeros_like(qt)
        qis_scr[:, (2 * p) * QA:(2 * p + 1) * QA] = jnp.where(sub_lo, qt, zero)
        qis_scr[:, (2 * p + 1) * QA:(2 * p + 2) * QA] = jnp.where(sub_lo, zero, qt)

    keys[0:PAD_A, :] = jnp.full((PAD_A, QA), INT_MIN, I32)
    gm_scr[...] = jnp.full((TI, QA), INT_MIN, I32)
    qcol = lax.broadcasted_iota(I32, (CB, QA), 1)
    krow = lax.broadcasted_iota(I32, (CB, QA), 0)

    def score_tile(tt, carry):
        base = pl.multiple_of(tt * TI, TI)
        x_scr[...] = _dot(ki_ref[0, pl.ds(base, TI), :], qis_scr[...])
        for cb in range(TI // CB):
            sc = jnp.zeros((CB, QA), F32)
            for h in range(IDX_HEADS):
                sc = sc + wit_ref[0, h:h + 1, :] * jnp.maximum(x_scr[cb * CB:(cb + 1) * CB, h * QA:(h + 1) * QA], 0.0)
            bits = lax.bitcast_convert_type(sc, I32)
            key = bits ^ ((bits >> 31) & 0x7FFFFFFF)
            adm = (krow + cb * CB < (qcol // CHUNK + 1) * CHUNK) | (tt <= i)
            key = jnp.where(adm, key, INT_MIN)
            keys[pl.ds(base + cb * CB, CB), :] = key
            gm_scr[cb * CB:(cb + 1) * CB, :] = jnp.maximum(gm_scr[cb * CB:(cb + 1) * CB, :], key)
        return carry

    lax.fori_loop(1, i + 2, score_tile, 0)

    gm = gm_scr[...]
    s_start = pl.multiple_of((i & 1) * PAD_A, PAD_A)
    s_ntile = (i + 2) // 2
    lo0 = jnp.min(gm, axis=0, keepdims=True)
    lo_scr[...] = lo0
    hi_scr[...] = jnp.max(gm, axis=0, keepdims=True) + 1
    cl_scr[...] = _count(keys, s_start, s_ntile, lambda kk, _: kk >= lo0)
    ch_scr[...] = jnp.zeros((1, QA), F32)

    def open_span():
        opn = hi_scr[...] - 1 > lo_scr[...]
        return jnp.max(jnp.where(opn, jnp.maximum(cl_scr[...] - ch_scr[...], 1.0), 0.0))

    def search(span):
        @pl.when(span <= SNAP_SPAN)
        def _():
            lo, hi = lo_scr[...], hi_scr[...]
            amin, amax = _snap(keys, s_start, s_ntile, lo, hi)
            opn = hi - 1 > lo
            lo_scr[...] = jnp.where(opn, amin, lo)
            hi_scr[...] = jnp.where(opn, amax + 1, hi)

        for _ in range(BISECT_UNROLL):
            lo, hi = lo_scr[...], hi_scr[...]
            mid = (lo >> 1) + (hi >> 1) + (lo & hi & 1)
            cnt = _count(keys, s_start, s_ntile, lambda kk, _: kk >= mid)
            ge = cnt >= topk
            exact = cnt == topk
            lo_scr[...] = jnp.where(ge, mid, lo)
            cl_scr[...] = jnp.where(ge, cnt, cl_scr[...])
            hi_scr[...] = jnp.where(exact, mid + 1, jnp.where(ge, hi, mid))
            ch_scr[...] = jnp.where(ge, ch_scr[...], cnt)
        return open_span()

    lax.while_loop(lambda s: s > 0.0, search, open_span())

    thr = lo_scr[...]
    tie = (cl_scr[...] > topk) & (thr > INT_MIN)

    @pl.when(jnp.max(jnp.where(tie, 1.0, 0.0)) > 0.0)
    def _():
        _demote_ties(keys, s_start, s_ntile, thr, (topk - ch_scr[...]).astype(I32), tie)

    thr = jnp.maximum(thr, INT_MIN + 1)

    m_scr[...] = jnp.full(m_scr.shape, NEG, F32)
    acc_scr[...] = jnp.zeros(acc_scr.shape, F32)

    def attend(start, with_bias):
        for cb in range(TA // CB):
            rows = pl.ds(start + cb * CB, CB)
            mb_scr[cb * CB:(cb + 1) * CB, :] = jnp.where(keys[rows, :] >= thr, 0.0, NEG)
        blk = start // TI

        def qk(g):
            s_scr[g % 2] = _dot(k_ref[0, pl.ds(start, TA), (g // 2) * LANE:(g // 2 + 1) * LANE], qs_scr[g])

        qk(0)
        for g in range(KV_GROUPS):
            buf = g % 2
            if g + 1 < KV_GROUPS:
                qk(g + 1)
            hds = [g * Q_PER_KV + r for r in range(Q_PER_KV)]

            def logits(kb):
                rows = slice(kb * KB, (kb + 1) * KB)
                mb = mb_scr[rows, :]
                out = []
                for r in range(Q_PER_KV):
                    s = s_scr[buf, rows, r * QA:(r + 1) * QA] + mb
                    if with_bias:
                        s = s + b_ref[hds[r], rows, :]
                    out.append(s)
                return out

            mx = logits(0)
            for kb in range(1, TA // KB):
                mx = [jnp.maximum(a, s) for a, s in zip(mx, logits(kb))]
            m_new = []
            for r in range(Q_PER_KV):
                m_old = m_scr[hds[r]]
                mn = jnp.maximum(m_old, jnp.max(mx[r], axis=0, keepdims=True))
                al_scr[hds[r]] = jnp.exp2(m_old - mn)
                m_scr[hds[r]] = mn
                m_new.append(jnp.broadcast_to(mn, (KB, QA)))
            for kb in range(TA // KB):
                ss = logits(kb)
                for r in range(Q_PER_KV):
                    p_scr[buf, kb * KB:(kb + 1) * KB, r * QA:(r + 1) * QA] = jnp.exp2(ss[r] - m_new[r]).astype(p_scr.dtype)
            for r in range(Q_PER_KV):
                cols = slice(r * QA, (r + 1) * QA)
                pv = _dot(vt_ref[0, g, blk], p_scr[buf, 0:TI, cols]) + _dot(vt_ref[0, g, blk + 1], p_scr[buf, TI:TA, cols])
                acc_scr[hds[r]] = al_scr[hds[r]] * acc_scr[hds[r]] + pv

    def far_body(t, carry):
        attend(pl.multiple_of((i & 1) * PAD_A + t * TA, PAD_A), False)
        return carry

    lax.fori_loop(0, i // 2, far_body, 0)
    attend(pl.multiple_of(i * QA, QA), True)

    for t in range(8):
        gp, r = divmod(t, 4)
        aa = acc_scr[(2 * gp) * Q_PER_KV + r]
        ab = acc_scr[(2 * gp + 1) * Q_PER_KV + r]
        oa = aa[:HEAD_DIM] / aa[HEAD_DIM:HEAD_DIM + 1]
        ob = ab[:HEAD_DIM] / ab[HEAD_DIM:HEAD_DIM + 1]
        o_ref[0, :, t * LANE:(t + 1) * LANE] = jnp.concatenate([oa, ob], axis=0).T.astype(o_ref.dtype)


def _attn_a(qt, qit, wit, kp, kip, vt, bias):
    b, d, s = qt.shape
    sp = s + PAD_A
    nh = d // HEAD_DIM
    return pl.pallas_call(
        _attn_a_kernel,
        grid=(b, s // QA),
        in_specs=[
            pl.BlockSpec((1, d, QA), lambda bi, i: (bi, 0, i)),
            pl.BlockSpec((1, qit.shape[1], QA), lambda bi, i: (bi, 0, i)),
            pl.BlockSpec((1, IDX_HEADS, QA), lambda bi, i: (bi, 0, i)),
            pl.BlockSpec((1, sp, kp.shape[-1]), lambda bi, i: (bi, 0, 0), pipeline_mode=pl.Buffered(1)),
            pl.BlockSpec((1, sp, LANE), lambda bi, i: (bi, 0, 0), pipeline_mode=pl.Buffered(1)),
            pl.BlockSpec((1,) + vt.shape[1:], lambda bi, i: (bi, 0, 0, 0, 0), pipeline_mode=pl.Buffered(1)),
            _const_spec(bias.shape),
        ],
        out_specs=pl.BlockSpec((1, QA, d), lambda bi, i: (bi, i, 0)),
        out_shape=jax.ShapeDtypeStruct((b, s, d), qt.dtype),
        scratch_shapes=[
            pltpu.VMEM((sp, QA), I32),
            pltpu.VMEM((TI, IDX_HEADS * QA), F32),
            pltpu.VMEM((2, TA, Q_PER_KV * QA), F32),
            pltpu.VMEM((2, TA, Q_PER_KV * QA), qt.dtype),
            pltpu.VMEM((KV_GROUPS, LANE, Q_PER_KV * QA), qt.dtype),
            pltpu.VMEM((LANE, IDX_HEADS * QA), qit.dtype),
            pltpu.VMEM((nh, VR, QA), F32),
            pltpu.VMEM((nh, 1, QA), F32),
            pltpu.VMEM((nh, 1, QA), F32),
            pltpu.VMEM((TA, QA), F32),
            pltpu.VMEM((TI, QA), I32),
            pltpu.VMEM((1, QA), I32),
            pltpu.VMEM((1, QA), I32),
            pltpu.VMEM((1, QA), F32),
            pltpu.VMEM((1, QA), F32),
        ],
        compiler_params=_cparams(("arbitrary", "arbitrary")),
        name="attn_sparse",
    )(qt, qit, wit, kp, kip, vt, bias)


HALO = 16


def _ffn_kernel(h_ref, hh_ref, m_ref, mh_ref, wo_ref, g_ref, wup_ref, cw_ref, cb_ref, wdn_ref, o_ref,
                me_scr, n_scr, h1_scr, u_scr, acc_scr, *, tm, dff, cw, seq):
    i = pl.program_id(0)
    first = (i * tm) % seq == 0
    me_scr[0:HALO] = mh_ref[...]
    me_scr[HALO:] = m_ref[...]
    h1_scr[...] = _dot(me_scr[...], wo_ref[...])
    h1_scr[0:HALO] = h1_scr[0:HALO] + hh_ref[...]
    h1_scr[HALO:] = h1_scr[HALO:] + h_ref[...]
    h1 = h1_scr[...]
    ms = jnp.mean(h1 * h1, axis=-1, keepdims=True)
    n = h1 * lax.rsqrt(ms + EPS) * g_ref[...]
    row = lax.broadcasted_iota(I32, n.shape, 0)
    n_scr[...] = jnp.where((row < HALO) & first, 0.0, n).astype(n_scr.dtype)
    acc_scr[...] = jnp.zeros(acc_scr.shape, F32)

    def up(c):
        for part in range(2):
            c0 = part * dff + c * cw
            u_scr[c % 2, part] = _dot(n_scr[...], wup_ref[:, c0:c0 + cw])

    up(0)
    for c in range(dff // cw):
        if c + 1 < dff // cw:
            up(c + 1)
        ys = []
        for part in range(2):
            c0 = part * dff + c * cw
            u = u_scr.at[c % 2, part]
            y = cb_ref[:, c0:c0 + cw] + cw_ref[0:1, c0:c0 + cw] * u[HALO - 2:HALO - 2 + tm, :]
            y = y + cw_ref[1:2, c0:c0 + cw] * u[HALO - 1:HALO - 1 + tm, :]
            y = y + cw_ref[2:3, c0:c0 + cw] * u[HALO:HALO + tm, :]
            ys.append(y)
        a, gte = ys
        act = (gte * (1.0 / (1.0 + jnp.exp(-gte)))) * a
        acc_scr[...] += _dot(act.astype(n_scr.dtype), wdn_ref[c * cw:(c + 1) * cw, :])
    o_ref[...] = h1_scr[HALO:] + acc_scr[...]


def _out_ffn(h, m, w_out, g, w_up, conv_w, conv_b, w_down, seq, tm=512, cw=256):
    n, d = h.shape
    dff = w_down.shape[0]
    hb = tm // HALO
    halo = lambda i: (jnp.maximum(i * hb - 1, 0), 0)
    return pl.pallas_call(
        functools.partial(_ffn_kernel, tm=tm, dff=dff, cw=cw, seq=seq),
        grid=(n // tm,),
        in_specs=[
            pl.BlockSpec((tm, d), lambda i: (i, 0)),
            pl.BlockSpec((HALO, d), halo),
            pl.BlockSpec((tm, d), lambda i: (i, 0)),
            pl.BlockSpec((HALO, d), halo),
            _const_spec((d, d)),
            _const_spec((1, d)),
            _const_spec((d, 2 * dff)),
            _const_spec((3, 2 * dff)),
            _const_spec((1, 2 * dff)),
            _const_spec((dff, d)),
        ],
        out_specs=pl.BlockSpec((tm, d), lambda i: (i, 0)),
        out_shape=jax.ShapeDtypeStruct((n, d), F32),
        scratch_shapes=[
            pltpu.VMEM((tm + HALO, d), BF16),
            pltpu.VMEM((tm + HALO, d), BF16),
            pltpu.VMEM((tm + HALO, d), F32),
            pltpu.VMEM((2, 2, tm + HALO, cw), F32),
            pltpu.VMEM((tm, d), F32),
        ],
        compiler_params=_cparams(("arbitrary",)),
        name="out_ffn",
    )(h, h, m, m, w_out.astype(BF16), g.astype(F32).reshape(1, d), w_up.astype(BF16),
      conv_w.astype(F32), conv_b.astype(F32).reshape(1, -1), w_down.astype(BF16))


_HEAD_PERM = np.array([8 * gp + 4 * half + r for gp in range(2) for r in range(4) for half in range(2)])
_COL_PERM = (_HEAD_PERM[:, None] * HEAD_DIM + np.arange(HEAD_DIM)[None, :]).reshape(-1)


def _pad_front(x, b, s, pad):
    return jnp.pad(x.reshape(b, s, x.shape[-1]), ((0, 0), (pad, 0), (0, 0)))


def _mixer_a(h2, b, s, g_attn, w_in, q_g, k_g, bias):
    ad = Q_PER_KV * KV_GROUPS * HEAD_DIM
    kd = KV_GROUPS * HEAD_DIM
    o_q, o_k, o_v, o_qi = 0, ad, ad + kd, ad + 2 * kd
    o_ki = o_qi + IDX_HEADS * IDX_DIM
    o_wi = o_ki + IDX_DIM
    w_ki = w_in[:, o_ki:o_wi]
    w_wi = jnp.pad(w_in[:, o_wi:o_wi + IDX_HEADS], ((0, 0), (0, LANE - IDX_HEADS)))
    wi_scale = IDX_HEADS ** -0.5 * IDX_DIM ** -0.5
    ones = lambda c: jnp.ones((c,), F32)
    segs = [
        (w_in[:, o_q:o_k][:, _COL_PERM], jnp.tile(q_g, ad // HEAD_DIM) * (HEAD_DIM ** -0.5 * LOG2E), True, BF16),
        (w_in[:, o_k:o_v], jnp.tile(k_g, KV_GROUPS), True, BF16),
        (w_in[:, o_v:o_qi], ones(kd), False, BF16),
        (w_in[:, o_qi:o_ki], ones(IDX_HEADS * IDX_DIM), False, BF16),
        (jnp.concatenate([w_ki, w_ki], axis=1), ones(LANE), True, BF16),
        (w_wi, ones(LANE) * wi_scale, False, F32),
    ]
    q, k, v, qi, ki, wi = _in_proj(h2, g_attn, segs)
    tr = lambda x: x.reshape(b, s, x.shape[-1]).transpose(0, 2, 1)
    sp = s + PAD_A
    vt = jnp.pad(tr(v), ((0, 0), (0, 0), (PAD_A, 0))).reshape(b, KV_GROUPS, HEAD_DIM, sp)
    vt = jnp.concatenate([vt, jnp.ones((b, KV_GROUPS, VR - HEAD_DIM, sp), vt.dtype)], axis=2)
    vt = vt.reshape(b, KV_GROUPS, VR, sp // TI, TI).transpose(0, 1, 3, 2, 4)
    o = _attn_a(tr(q), tr(qi), tr(wi[:, :IDX_HEADS]), _pad_front(k, b, s, PAD_A), _pad_front(ki, b, s, PAD_A),
                vt, bias)
    return o.reshape(b * s, ad)


def _mixer_b(h2, b, s, g_attn, w_in, q_g, k_g, bias):
    d = w_in.shape[1] // 3
    nh = d // HEAD_DIM
    segs = [
        (w_in[:, :d], jnp.tile(q_g, nh) * (HEAD_DIM ** -0.5 * LOG2E), True, BF16),
        (w_in[:, d:2 * d], jnp.tile(k_g, nh), True, BF16),
        (w_in[:, 2 * d:], jnp.ones((d,), F32), False, BF16),
    ]
    q, k, v = _in_proj(h2, g_attn, segs)
    o = _attn_b(q.reshape(b, s, d), _pad_front(k, b, s, PAD_B), _pad_front(v, b, s, PAD_B), bias)
    return o.reshape(b * s, d)


def kernel(x, attn_norm_g, w_in_a, w_in_b, q_norm_g, k_norm_g, t5_bias, rel_bias_b, w_out, ffn_norm_g, w_up, conv_w, conv_b, w_down):
    b, s, d = x.shape
    depth = attn_norm_g.shape[0]
    assert s % SB == 0 and s % QA == 0 and min(TOPK_MAX, s // 4) == TOPK_MAX
    h = x.reshape(b * s, d)
    bias_a = _bias_a(t5_bias)
    for i in range(depth):
        if i % 2 == 0:
            m = _mixer_a(h, b, s, attn_norm_g[i], w_in_a[i // 2], q_norm_g[i], k_norm_g[i], bias_a)
            wo = w_out[i][_COL_PERM, :]
        else:
            m = _mixer_b(h, b, s, attn_norm_g[i], w_in_b[i // 2], q_norm_g[i], k_norm_g[i], _bias_b(rel_bias_b[i // 2]))
            wo = w_out[i]
        h = _out_ffn(h, m, wo, ffn_norm_g[i], w_up[i], conv_w[i], conv_b[i], w_down[i], s)
    return h.reshape(b, s, d)
```

```python
import functools

import numpy as np
import jax
import jax.numpy as jnp
from jax import lax
from jax.experimental import pallas as pl
from jax.experimental.pallas import tpu as pltpu

F32 = jnp.float32
BF16 = jnp.bfloat16
I32 = jnp.int32

EPS = 1e-6
CHUNK = 64
HEAD_DIM = 64
KV_GROUPS = 4
Q_PER_KV = 4
IDX_HEADS = 8
IDX_DIM = 64
TOPK_MAX = 256
T5_BUCKETS = 32
LEFT_CHUNKS = 8
REL_CLIP = 256

LANE = 128
SUBLANE = 8
V7X_VMEM_BYTES = 64 * 2**20
VMEM_LIMIT = 56 * 2**20

NEG = -1e30
LOG2E = 1.4426950408889634
INT_MIN = -2**31
INT_MAX = 2**31 - 1

QA = 256
PAD_A = 256
TI = 256
TA = 512
VR = HEAD_DIM + 16
CB = 64
KB = 16
BISECT_UNROLL = 2
SNAP_SPAN = 8.0
QB = 128
SB = 1024
PAD_B = LEFT_CHUNKS * CHUNK
WIN_B = PAD_B + QB
UNROLL_B = 8


def _cparams(sem):
    return pltpu.CompilerParams(dimension_semantics=sem, vmem_limit_bytes=VMEM_LIMIT)


def _const_spec(shape):
    nd = len(shape)
    return pl.BlockSpec(shape, lambda *_: (0,) * nd, pipeline_mode=pl.Buffered(1))


def _lo_half(shape):
    return (lax.broadcasted_iota(I32, shape, len(shape) - 1) & HEAD_DIM) == 0


def _dot_t(a, b):
    return lax.dot_general(a, b, (((1,), (1,)), ((), ())), preferred_element_type=F32)


def _dot(a, b):
    return jnp.dot(a, b, preferred_element_type=F32)


def _head_rms(y):
    lo = _lo_half(y.shape)
    z = y * y
    sa = jnp.sum(jnp.where(lo, z, 0.0), axis=-1, keepdims=True)
    sb = jnp.sum(jnp.where(lo, 0.0, z), axis=-1, keepdims=True)
    inv = jnp.where(lo, lax.rsqrt(sa * (1.0 / HEAD_DIM) + EPS), lax.rsqrt(sb * (1.0 / HEAD_DIM) + EPS))
    return y * inv


def _proj_kernel(x_ref, g_ref, *rest, segs, cw):
    ns = len(segs)
    w_refs, s_refs, o_refs = rest[:ns], rest[ns:2 * ns], rest[2 * ns:3 * ns]
    n_scr = rest[3 * ns]
    x = x_ref[...]
    ms = jnp.mean(x * x, axis=-1, keepdims=True)
    n_scr[...] = (x * lax.rsqrt(ms + EPS) * g_ref[...]).astype(n_scr.dtype)
    for (cols, headnorm), w_ref, s_ref, o_ref in zip(segs, w_refs, s_refs, o_refs):
        for c0 in range(0, cols, cw):
            c1 = min(c0 + cw, cols)
            y = _dot(n_scr[...], w_ref[:, c0:c1])
            for l0 in range(0, c1 - c0, LANE):
                yl = y[:, l0:l0 + LANE]
                if headnorm:
                    yl = _head_rms(yl)
                o_ref[:, c0 + l0:c0 + l0 + LANE] = (yl * s_ref[:, c0 + l0:c0 + l0 + LANE]).astype(o_ref.dtype)


def _in_proj(x, g, segs, tm=512, cw=256):
    n, d = x.shape
    ws = [s[0].astype(BF16) for s in segs]
    ss = [s[1].astype(F32).reshape(1, -1) for s in segs]
    meta = tuple((int(s[0].shape[1]), bool(s[2])) for s in segs)
    in_specs = [pl.BlockSpec((tm, d), lambda i: (i, 0)), _const_spec((1, d))]
    in_specs += [_const_spec(w.shape) for w in ws] + [_const_spec(s.shape) for s in ss]
    out_specs = [pl.BlockSpec((tm, c), lambda i: (i, 0)) for c, _ in meta]
    out_shape = [jax.ShapeDtypeStruct((n, c), s[3]) for (c, _), s in zip(meta, segs)]
    return pl.pallas_call(
        functools.partial(_proj_kernel, segs=meta, cw=cw),
        grid=(n // tm,),
        in_specs=in_specs,
        out_specs=out_specs,
        out_shape=out_shape,
        scratch_shapes=[pltpu.VMEM((tm, d), BF16)],
        compiler_params=_cparams(("arbitrary",)),
        name="in_proj",
    )(x, g.astype(F32).reshape(1, d), *ws, *ss)


def _toeplitz_kernel(vec_ref, off_ref, o_ref, *, rows, cols, band):
    n = vec_ref.shape[-1]
    x = jnp.broadcast_to(vec_ref[0], (rows, n))
    t = (pltpu.roll(x, 0, 1, stride=1, stride_axis=0)[:, :cols] - off_ref[0][:, :1]) * LOG2E
    if band is not None:
        r = lax.broadcasted_iota(I32, (rows, cols), 0)
        c = lax.broadcasted_iota(I32, (rows, cols), 1)
        lo = (r // CHUNK) * CHUNK
        t = jnp.where((c >= lo) & (c < lo + band), t, NEG)
    o_ref[0] = t


def _toeplitz(vec, off, rows, cols, band=None):
    h, n = vec.shape
    return pl.pallas_call(
        functools.partial(_toeplitz_kernel, rows=rows, cols=cols, band=band),
        grid=(h,),
        in_specs=[pl.BlockSpec((1, 1, n), lambda i: (i, 0, 0)), pl.BlockSpec((1, 1, LANE), lambda i: (i, 0, 0))],
        out_specs=pl.BlockSpec((1, rows, cols), lambda i: (i, 0, 0)),
        out_shape=jax.ShapeDtypeStruct((h, rows, cols), F32),
        compiler_params=_cparams(("arbitrary",)),
        name="toeplitz_bias",
    )(vec.reshape(h, 1, n), jnp.broadcast_to(off.reshape(h, 1, 1), (h, 1, LANE)))


def _t5_bucket_static(rel):
    n = np.abs(rel)
    large = 8 + sum((n >= t).astype(np.int64) for t in (12, 16, 23, 32, 46, 64, 91))
    return np.where(rel > 0, 16, 0) + np.where(n < 8, n, large)


def _bias_a(t5_table):
    n = 2 * TA
    j = np.arange(n)
    d = np.where(j < n // 2, j, j - n)
    bucket = _t5_bucket_static(-d - PAD_A)
    vec = t5_table[bucket, :].T.astype(F32)
    far = t5_table[T5_BUCKETS // 2 - 1, :].astype(F32)
    return _toeplitz(vec, far, TA, QA)


def _bias_b(rel_table):
    n = 1024
    j = np.arange(n)
    d = np.where(j < WIN_B, j, j - n)
    idx = np.clip(PAD_B - d, -REL_CLIP, REL_CLIP) + REL_CLIP
    vec = rel_table[idx, :].T.astype(F32)
    return _toeplitz(vec, jnp.zeros((vec.shape[0],), F32), QB, WIN_B, band=PAD_B + CHUNK)


def _attn_b_kernel(q_ref, k_ref, v_ref, b_ref, o_ref):
    j = pl.program_id(2)
    lo = _lo_half((QB, LANE))
    bias = b_ref[...].reshape(2 * QB, WIN_B)
    col = lax.broadcasted_iota(I32, (2 * QB, WIN_B), 1)

    def block(t):
        q0 = pl.multiple_of(j * SB + t * QB, QB)
        qs = q_ref[0, pl.ds(pl.multiple_of(t * QB, QB), QB), :]
        zero = jnp.zeros_like(qs)
        qq = jnp.concatenate([jnp.where(lo, qs, zero), jnp.where(lo, zero, qs)], axis=0)
        s = _dot_t(qq, k_ref[0, pl.ds(q0, WIN_B), :]) + bias
        s = jnp.where(col >= PAD_B - q0, s, NEG)
        m = jnp.max(s, axis=-1, keepdims=True)
        p = jnp.exp2(s - m)
        l = jnp.sum(p, axis=-1, keepdims=True)
        pv = _dot(p.astype(v_ref.dtype), v_ref[0, pl.ds(q0, WIN_B), :]) / l
        o_ref[0, pl.ds(pl.multiple_of(t * QB, QB), QB), :] = jnp.where(lo, pv[:QB], pv[QB:]).astype(o_ref.dtype)

    def body(t2, carry):
        for k in range(UNROLL_B):
            block(t2 * UNROLL_B + k)
        return carry

    lax.fori_loop(0, SB // QB // UNROLL_B, body, 0)


def _attn_b(q, kp, vp, bias):
    b, s, d = q.shape
    npair = d // LANE
    return pl.pallas_call(
        _attn_b_kernel,
        grid=(b, npair, s // SB),
        in_specs=[
            pl.BlockSpec((1, SB, LANE), lambda bi, p, j: (bi, j, p)),
            pl.BlockSpec((1, s + PAD_B, LANE), lambda bi, p, j: (bi, 0, p)),
            pl.BlockSpec((1, s + PAD_B, LANE), lambda bi, p, j: (bi, 0, p)),
            pl.BlockSpec((2, QB, WIN_B), lambda bi, p, j: (p, 0, 0)),
        ],
        out_specs=pl.BlockSpec((1, SB, LANE), lambda bi, p, j: (bi, j, p)),
        out_shape=jax.ShapeDtypeStruct((b, s, d), q.dtype),
        compiler_params=_cparams(("arbitrary", "arbitrary", "arbitrary")),
        name="attn_band",
    )(q, kp, vp, bias)


KEY_FINITE_MAX = 0x7F7FFFFF
KEY_FINITE_MIN = -0x7F800000


def _score_to_key(x):
    bits = lax.bitcast_convert_type(x, I32)
    return bits ^ ((bits >> 31) & 0x7FFFFFFF)


def _key_to_score(k):
    k = jnp.minimum(jnp.maximum(k, KEY_FINITE_MIN), KEY_FINITE_MAX)
    return lax.bitcast_convert_type(k ^ ((k >> 31) & 0x7FFFFFFF), F32)


def _count(keys, start, ntile, pred):
    nacc = 4

    def tile(t, accs):
        accs = list(accs)
        base = start + t * TA
        for cb in range(TA // CB):
            blk = keys[pl.ds(base + cb * CB, CB), :]
            for c in range(CB // SUBLANE):
                kk = blk[c * SUBLANE:(c + 1) * SUBLANE]
                a = accs[c % nacc]
                accs[c % nacc] = jnp.where(pred(kk, base + cb * CB + c * SUBLANE), a + 1, a)
        return tuple(accs)

    accs = lax.fori_loop(0, ntile, tile, tuple(jnp.zeros((SUBLANE, QA), I32) for _ in range(nacc)))
    tot = (accs[0] + accs[1]) + (accs[2] + accs[3])
    return jnp.sum(tot, axis=0, keepdims=True).astype(F32)


def _snap(keys, start, ntile, lo, hi):
    def tile(t, accs):
        mn, mx = list(accs[:2]), list(accs[2:])
        base = start + t * TA
        for cb in range(TA // CB):
            blk = keys[pl.ds(base + cb * CB, CB), :]
            for c in range(CB // SUBLANE):
                kk = blk[c * SUBLANE:(c + 1) * SUBLANE]
                mn[c % 2] = jnp.minimum(mn[c % 2], jnp.where(kk >= lo, kk, INT_MAX))
                mx[c % 2] = jnp.maximum(mx[c % 2], jnp.where(kk < hi, kk, INT_MIN))
        return tuple(mn + mx)

    top = jnp.full((SUBLANE, QA), INT_MAX, I32)
    bot = jnp.full((SUBLANE, QA), INT_MIN, I32)
    mn0, mn1, mx0, mx1 = lax.fori_loop(0, ntile, tile, (top, top, bot, bot))
    return (jnp.min(jnp.minimum(mn0, mn1), axis=0, keepdims=True),
            jnp.max(jnp.maximum(mx0, mx1), axis=0, keepdims=True))


def _demote_ties(keys, start, ntile, thr, keep, tie):
    rid = lax.broadcasted_iota(I32, (SUBLANE, QA), 0)

    def block(c, seen):
        r0 = start + c * CB
        blk = keys[pl.ds(r0, CB), :]
        out = []
        for s in range(CB // SUBLANE):
            kk = blk[s * SUBLANE:(s + 1) * SUBLANE]
            eq = (kk == thr) & tie
            e = jnp.where(eq, 1, 0)
            for sh in (1, 2, 4):
                e = e + jnp.where(rid >= sh, pltpu.roll(e, sh, 0), 0)
            out.append(jnp.where(eq & (seen + e > keep), INT_MIN, kk))
            seen = seen + e[SUBLANE - 1:SUBLANE]
        keys[pl.ds(r0, CB), :] = jnp.concatenate(out, axis=0)
        return seen

    lax.fori_loop(0, ntile * (TA // CB), block, jnp.zeros((1, QA), I32))


def _attn_a_kernel(qt_ref, qit_ref, wit_ref, k_ref, ki_ref, vt_ref, b_ref, o_ref,
                   keys, x0_scr, x1_scr, s_scr, p_scr, qs_scr, qis_scr, acc_scr, m_scr, al_scr, mb_scr, gm_scr,
                   lo_scr, hi_scr, cl_scr, ch_scr):
    i = pl.program_id(1)
    topk = float(TOPK_MAX)
    sub_lo = lax.broadcasted_iota(I32, (LANE, QA), 0) < HEAD_DIM

    for t in range(8):
        qt = qt_ref[0, t * LANE:(t + 1) * LANE, :]
        gp, r = divmod(t, 4)
        zero = jnp.zeros_like(qt)
        qs_scr[2 * gp, :, r * QA:(r + 1) * QA] = jnp.where(sub_lo, qt, zero)
        qs_scr[2 * gp + 1, :, r * QA:(r + 1) * QA] = jnp.where(sub_lo, zero, qt)
    for p in range(IDX_HEADS // 2):
        qt = qit_ref[0, p * LANE:(p + 1) * LANE, :]
        zero = jnp.zeros_like(qt)
        qis_scr[:, (2 * p) * QA:(2 * p + 1) * QA] = jnp.where(sub_lo, qt, zero)
        qis_scr[:, (2 * p + 1) * QA:(2 * p + 2) * QA] = jnp.where(sub_lo, zero, qt)

    gm_scr[...] = jnp.full((TI, QA), INT_MIN, I32)
    qcol = lax.broadcasted_iota(I32, (CB, QA), 1)
    krow = lax.broadcasted_iota(I32, (CB, QA), 0)
    last = i + 1
    x_scr = (x0_scr, x1_scr)

    def score_dot(tt, buf):
        x_scr[buf][...] = _dot(ki_ref[0, pl.ds(pl.multiple_of(tt * TI, TI), TI), :], qis_scr[...])

    def score_keys(tt, buf):
        base = pl.multiple_of(tt * TI, TI)
        for cb in range(TI // CB):
            sc = jnp.zeros((CB, QA), F32)
            for h in range(IDX_HEADS):
                sc = sc + wit_ref[0, h:h + 1, :] * jnp.maximum(x_scr[buf][cb * CB:(cb + 1) * CB, h * QA:(h + 1) * QA], 0.0)
            causal = (krow + cb * CB < (qcol // CHUNK + 1) * CHUNK) | (tt < last)
            key = jnp.where(causal & (tt > 0), _score_to_key(sc), INT_MIN)
            keys[pl.ds(base + cb * CB, CB), :] = key
            gm_scr[cb * CB:(cb + 1) * CB, :] = jnp.maximum(gm_scr[cb * CB:(cb + 1) * CB, :], key)

    first = i & 1
    score_dot(first, 0)

    def score_pair(p, carry):
        ta = first + 2 * p
        score_dot(ta + 1, 1)
        score_keys(ta, 0)
        score_dot(jnp.minimum(ta + 2, last), 0)
        score_keys(ta + 1, 1)
        return carry

    lax.fori_loop(0, (i + 2 - first) // 2, score_pair, 0)

    gm = gm_scr[...]
    s_start = pl.multiple_of((i & 1) * PAD_A, PAD_A)
    s_ntile = (i + 2) // 2
    lo0 = jnp.min(gm, axis=0, keepdims=True)
    lo_scr[...] = lo0
    hi_scr[...] = jnp.max(gm, axis=0, keepdims=True) + 1
    cl_scr[...] = _count(keys, s_start, s_ntile, lambda kk, _: kk >= lo0)
    ch_scr[...] = jnp.zeros((1, QA), F32)

    def open_span():
        opn = hi_scr[...] - 1 > lo_scr[...]
        return jnp.max(jnp.where(opn, jnp.maximum(cl_scr[...] - ch_scr[...], 1.0), 0.0))

    def search(state):
        span, prev = state

        @pl.when((span <= SNAP_SPAN) | (span >= prev))
        def _():
            lo, hi = lo_scr[...], hi_scr[...]
            amin, amax = _snap(keys, s_start, s_ntile, lo, hi)
            opn = hi - 1 > lo
            lo_scr[...] = jnp.where(opn, amin, lo)
            hi_scr[...] = jnp.where(opn, amax + 1, hi)

        for by_value in (True, False):
            lo, hi = lo_scr[...], hi_scr[...]
            cl, ch = cl_scr[...], ch_scr[...]
            if by_value:
                fm = 0.5 * _key_to_score(lo) + 0.5 * _key_to_score(hi)
                mid = jnp.where(hi - 1 > lo, jnp.minimum(jnp.maximum(_score_to_key(fm), lo + 1), hi - 1), lo)
            else:
                mid = (lo >> 1) + (hi >> 1) + (lo & hi & 1)
            cnt = _count(keys, s_start, s_ntile, lambda kk, _: kk >= mid)
            ge = cnt >= topk
            exact = cnt == topk
            lo_scr[...] = jnp.where(ge, mid, lo)
            cl_scr[...] = jnp.where(ge, cnt, cl)
            hi_scr[...] = jnp.where(exact, mid + 1, jnp.where(ge, hi, mid))
            ch_scr[...] = jnp.where(ge, ch, cnt)
        return open_span(), span

    lax.while_loop(lambda st: st[0] > 0.0, search, (open_span(), jnp.float32(3e38)))

    thr = lo_scr[...]
    tie = (cl_scr[...] > topk) & (thr > INT_MIN)

    @pl.when(jnp.max(jnp.where(tie, 1.0, 0.0)) > 0.0)
    def _():
        _demote_ties(keys, s_start, s_ntile, thr, (topk - ch_scr[...]).astype(I32), tie)

    thr = jnp.maximum(thr, INT_MIN + 1)

    m_scr[...] = jnp.full(m_scr.shape, NEG, F32)
    acc_scr[...] = jnp.zeros(acc_scr.shape, F32)

    def attend(start, with_bias):
        for cb in range(TA // CB):
            rows = pl.ds(start + cb * CB, CB)
            mb_scr[cb * CB:(cb + 1) * CB, :] = jnp.where(keys[rows, :] >= thr, 0.0, NEG)
        blk = start // TI

        def qk(g):
            s_scr[g % 2] = _dot(k_ref[0, pl.ds(start, TA), (g // 2) * LANE:(g // 2 + 1) * LANE], qs_scr[g])

        qk(0)
        for g in range(KV_GROUPS):
            buf = g % 2
            if g + 1 < KV_GROUPS:
                qk(g + 1)
            hds = [g * Q_PER_KV + r for r in range(Q_PER_KV)]

            def logits(kb):
                rows = slice(kb * KB, (kb + 1) * KB)
                mb = mb_scr[rows, :]
                out = []
                for r in range(Q_PER_KV):
                    s = s_scr[buf, rows, r * QA:(r + 1) * QA] + mb
                    if with_bias:
                        s = s + b_ref[hds[r], rows, :]
                    out.append(s)
                return out

            mx = logits(0)
            for kb in range(1, TA // KB):
                mx = [jnp.maximum(a, s) for a, s in zip(mx, logits(kb))]
            m_new = []
            for r in range(Q_PER_KV):
                m_old = m_scr[hds[r]]
                mn = jnp.maximum(m_old, jnp.max(mx[r], axis=0, keepdims=True))
                al_scr[hds[r]] = jnp.exp2(m_old - mn)
                m_scr[hds[r]] = mn
                m_new.append(jnp.broadcast_to(mn, (KB, QA)))
            for kb in range(TA // KB):
                ss = logits(kb)
                for r in range(Q_PER_KV):
                    p_scr[buf, kb * KB:(kb + 1) * KB, r * QA:(r + 1) * QA] = jnp.exp2(ss[r] - m_new[r]).astype(p_scr.dtype)
            for r in range(Q_PER_KV):
                cols = slice(r * QA, (r + 1) * QA)
                pv = _dot(vt_ref[0, g, blk], p_scr[buf, 0:TI, cols]) + _dot(vt_ref[0, g, blk + 1], p_scr[buf, TI:TA, cols])
                acc_scr[hds[r]] = al_scr[hds[r]] * acc_scr[hds[r]] + pv

    def far_body(t, carry):
        attend(pl.multiple_of((i & 1) * PAD_A + t * TA, PAD_A), False)
        return carry

    lax.fori_loop(0, i // 2, far_body, 0)
    attend(pl.multiple_of(i * QA, QA), True)

    for t in range(8):
        gp, r = divmod(t, 4)
        aa = acc_scr[(2 * gp) * Q_PER_KV + r]
        ab = acc_scr[(2 * gp + 1) * Q_PER_KV + r]
        oa = aa[:HEAD_DIM] / aa[HEAD_DIM:HEAD_DIM + 1]
        ob = ab[:HEAD_DIM] / ab[HEAD_DIM:HEAD_DIM + 1]
        o_ref[0, :, t * LANE:(t + 1) * LANE] = jnp.concatenate([oa, ob], axis=0).T.astype(o_ref.dtype)


def _attn_a(qt, qit, wit, kp, kip, vt, bias):
    b, d, s = qt.shape
    sp = s + PAD_A
    nh = d // HEAD_DIM
    return pl.pallas_call(
        _attn_a_kernel,
        grid=(b, s // QA),
        in_specs=[
            pl.BlockSpec((1, d, QA), lambda bi, i: (bi, 0, i)),
            pl.BlockSpec((1, qit.shape[1], QA), lambda bi, i: (bi, 0, i)),
            pl.BlockSpec((1, IDX_HEADS, QA), lambda bi, i: (bi, 0, i)),
            pl.BlockSpec((1, sp, kp.shape[-1]), lambda bi, i: (bi, 0, 0), pipeline_mode=pl.Buffered(1)),
            pl.BlockSpec((1, sp, LANE), lambda bi, i: (bi, 0, 0), pipeline_mode=pl.Buffered(1)),
            pl.BlockSpec((1,) + vt.shape[1:], lambda bi, i: (bi, 0, 0, 0, 0), pipeline_mode=pl.Buffered(1)),
            _const_spec(bias.shape),
        ],
        out_specs=pl.BlockSpec((1, QA, d), lambda bi, i: (bi, i, 0)),
        out_shape=jax.ShapeDtypeStruct((b, s, d), qt.dtype),
        scratch_shapes=[
            pltpu.VMEM((sp, QA), I32),
            pltpu.VMEM((TI, IDX_HEADS * QA), F32),
            pltpu.VMEM((TI, IDX_HEADS * QA), F32),
            pltpu.VMEM((2, TA, Q_PER_KV * QA), F32),
            pltpu.VMEM((2, TA, Q_PER_KV * QA), qt.dtype),
            pltpu.VMEM((KV_GROUPS, LANE, Q_PER_KV * QA), qt.dtype),
            pltpu.VMEM((LANE, IDX_HEADS * QA), qit.dtype),
            pltpu.VMEM((nh, VR, QA), F32),
            pltpu.VMEM((nh, 1, QA), F32),
            pltpu.VMEM((nh, 1, QA), F32),
            pltpu.VMEM((TA, QA), F32),
            pltpu.VMEM((TI, QA), I32),
            pltpu.VMEM((1, QA), I32),
            pltpu.VMEM((1, QA), I32),
            pltpu.VMEM((1, QA), F32),
            pltpu.VMEM((1, QA), F32),
        ],
        compiler_params=_cparams(("arbitrary", "arbitrary")),
        name="attn_sparse",
    )(qt, qit, wit, kp, kip, vt, bias)


HALO = 16


def _ffn_kernel(h_ref, hh_ref, m_ref, mh_ref, wo_ref, g_ref, wup_ref, cw_ref, cb_ref, wdn_ref, o_ref,
                me_scr, n_scr, h1_scr, u_scr, acc_scr, *, tm, dff, cw, seq):
    i = pl.program_id(0)
    first = (i * tm) % seq == 0
    me_scr[0:HALO] = mh_ref[...]
    me_scr[HALO:] = m_ref[...]
    h1_scr[...] = _dot(me_scr[...], wo_ref[...])
    h1_scr[0:HALO] = h1_scr[0:HALO] + hh_ref[...]
    h1_scr[HALO:] = h1_scr[HALO:] + h_ref[...]
    h1 = h1_scr[...]
    ms = jnp.mean(h1 * h1, axis=-1, keepdims=True)
    n = h1 * lax.rsqrt(ms + EPS) * g_ref[...]
    row = lax.broadcasted_iota(I32, n.shape, 0)
    n_scr[...] = jnp.where((row < HALO) & first, 0.0, n).astype(n_scr.dtype)
    acc_scr[...] = jnp.zeros(acc_scr.shape, F32)

    def up(c):
        for part in range(2):
            c0 = part * dff + c * cw
            u_scr[c % 2, part] = _dot(n_scr[...], wup_ref[:, c0:c0 + cw])

    up(0)
    for c in range(dff // cw):
        if c + 1 < dff // cw:
            up(c + 1)
        ys = []
        for part in range(2):
            c0 = part * dff + c * cw
            u = u_scr.at[c % 2, part]
            y = cb_ref[:, c0:c0 + cw] + cw_ref[0:1, c0:c0 + cw] * u[HALO - 2:HALO - 2 + tm, :]
            y = y + cw_ref[1:2, c0:c0 + cw] * u[HALO - 1:HALO - 1 + tm, :]
            y = y + cw_ref[2:3, c0:c0 + cw] * u[HALO:HALO + tm, :]
            ys.append(y)
        a, gte = ys
        act = (gte * (1.0 / (1.0 + jnp.exp(-gte)))) * a
        acc_scr[...] += _dot(act.astype(n_scr.dtype), wdn_ref[c * cw:(c + 1) * cw, :])
    o_ref[...] = h1_scr[HALO:] + acc_scr[...]


def _out_ffn(h, m, w_out, g, w_up, conv_w, conv_b, w_down, seq, tm=512, cw=256):
    n, d = h.shape
    dff = w_down.shape[0]
    hb = tm // HALO
    halo = lambda i: (jnp.maximum(i * hb - 1, 0), 0)
    return pl.pallas_call(
        functools.partial(_ffn_kernel, tm=tm, dff=dff, cw=cw, seq=seq),
        grid=(n // tm,),
        in_specs=[
            pl.BlockSpec((tm, d), lambda i: (i, 0)),
            pl.BlockSpec((HALO, d), halo),
            pl.BlockSpec((tm, d), lambda i: (i, 0)),
            pl.BlockSpec((HALO, d), halo),
            _const_spec((d, d)),
            _const_spec((1, d)),
            _const_spec((d, 2 * dff)),
            _const_spec((3, 2 * dff)),
            _const_spec((1, 2 * dff)),
            _const_spec((dff, d)),
        ],
        out_specs=pl.BlockSpec((tm, d), lambda i: (i, 0)),
        out_shape=jax.ShapeDtypeStruct((n, d), F32),
        scratch_shapes=[
            pltpu.VMEM((tm + HALO, d), BF16),
            pltpu.VMEM((tm + HALO, d), BF16),
            pltpu.VMEM((tm + HALO, d), F32),
            pltpu.VMEM((2, 2, tm + HALO, cw), F32),
            pltpu.VMEM((tm, d), F32),
        ],
        compiler_params=_cparams(("arbitrary",)),
        name="out_ffn",
    )(h, h, m, m, w_out.astype(BF16), g.astype(F32).reshape(1, d), w_up.astype(BF16),
      conv_w.astype(F32), conv_b.astype(F32).reshape(1, -1), w_down.astype(BF16))


_HEAD_PERM = np.array([8 * gp + 4 * half + r for gp in range(2) for r in range(4) for half in range(2)])
_COL_PERM = (_HEAD_PERM[:, None] * HEAD_DIM + np.arange(HEAD_DIM)[None, :]).reshape(-1)


def _pad_front(x, b, s, pad):
    return jnp.pad(x.reshape(b, s, x.shape[-1]), ((0, 0), (pad, 0), (0, 0)))


def _mixer_a(h2, b, s, g_attn, w_in, q_g, k_g, bias):
    ad = Q_PER_KV * KV_GROUPS * HEAD_DIM
    kd = KV_GROUPS * HEAD_DIM
    o_q, o_k, o_v, o_qi = 0, ad, ad + kd, ad + 2 * kd
    o_ki = o_qi + IDX_HEADS * IDX_DIM
    o_wi = o_ki + IDX_DIM
    w_ki = w_in[:, o_ki:o_wi]
    w_wi = jnp.pad(w_in[:, o_wi:o_wi + IDX_HEADS], ((0, 0), (0, LANE - IDX_HEADS)))
    wi_scale = IDX_HEADS ** -0.5 * IDX_DIM ** -0.5
    ones = lambda c: jnp.ones((c,), F32)
    segs = [
        (w_in[:, o_q:o_k][:, _COL_PERM], jnp.tile(q_g, ad // HEAD_DIM) * (HEAD_DIM ** -0.5 * LOG2E), True, BF16),
        (w_in[:, o_k:o_v], jnp.tile(k_g, KV_GROUPS), True, BF16),
        (w_in[:, o_v:o_qi], ones(kd), False, BF16),
        (w_in[:, o_qi:o_ki], ones(IDX_HEADS * IDX_DIM), False, BF16),
        (jnp.concatenate([w_ki, w_ki], axis=1), ones(LANE), True, BF16),
        (w_wi, ones(LANE) * wi_scale, False, F32),
    ]
    q, k, v, qi, ki, wi = _in_proj(h2, g_attn, segs)
    tr = lambda x: x.reshape(b, s, x.shape[-1]).transpose(0, 2, 1)
    sp = s + PAD_A
    vt = jnp.pad(tr(v), ((0, 0), (0, 0), (PAD_A, 0))).reshape(b, KV_GROUPS, HEAD_DIM, sp)
    vt = jnp.concatenate([vt, jnp.ones((b, KV_GROUPS, VR - HEAD_DIM, sp), vt.dtype)], axis=2)
    vt = vt.reshape(b, KV_GROUPS, VR, sp // TI, TI).transpose(0, 1, 3, 2, 4)
    o = _attn_a(tr(q), tr(qi), tr(wi[:, :IDX_HEADS]), _pad_front(k, b, s, PAD_A), _pad_front(ki, b, s, PAD_A),
                vt, bias)
    return o.reshape(b * s, ad)


def _mixer_b(h2, b, s, g_attn, w_in, q_g, k_g, bias):
    d = w_in.shape[1] // 3
    nh = d // HEAD_DIM
    segs = [
        (w_in[:, :d], jnp.tile(q_g, nh) * (HEAD_DIM ** -0.5 * LOG2E), True, BF16),
        (w_in[:, d:2 * d], jnp.tile(k_g, nh), True, BF16),
        (w_in[:, 2 * d:], jnp.ones((d,), F32), False, BF16),
    ]
    q, k, v = _in_proj(h2, g_attn, segs)
    o = _attn_b(q.reshape(b, s, d), _pad_front(k, b, s, PAD_B), _pad_front(v, b, s, PAD_B), bias)
    return o.reshape(b * s, d)


def kernel(x, attn_norm_g, w_in_a, w_in_b, q_norm_g, k_norm_g, t5_bias, rel_bias_b, w_out, ffn_norm_g, w_up, conv_w, conv_b, w_down):
    b, s, d = x.shape
    depth = attn_norm_g.shape[0]
    assert s % SB == 0 and s % QA == 0 and min(TOPK_MAX, s // 4) == TOPK_MAX
    h = x.reshape(b * s, d)
    bias_a = _bias_a(t5_bias)
    for i in range(depth):
        if i % 2 == 0:
            m = _mixer_a(h, b, s, attn_norm_g[i], w_in_a[i // 2], q_norm_g[i], k_norm_g[i], bias_a)
            wo = w_out[i][_COL_PERM, :]
        else:
            m = _mixer_b(h, b, s, attn_norm_g[i], w_in_b[i // 2], q_norm_g[i], k_norm_g[i], _bias_b(rel_bias_b[i // 2]))
            wo = w_out[i]
        h = _out_ffn(h, m, wo, ffn_norm_g[i], w_up[i], conv_w[i], conv_b[i], w_down[i], s)
    return h.reshape(b, s, d)
```

```python
import functools

import numpy as np
import jax
import jax.numpy as jnp
from jax import lax
from jax.experimental import pallas as pl
from jax.experimental.pallas import tpu as pltpu

F32 = jnp.float32
BF16 = jnp.bfloat16
I32 = jnp.int32

EPS = 1e-6
CHUNK = 64
HEAD_DIM = 64
KV_GROUPS = 4
Q_PER_KV = 4
IDX_HEADS = 8
IDX_DIM = 64
TOPK_MAX = 256
T5_BUCKETS = 32
LEFT_CHUNKS = 8
REL_CLIP = 256

LANE = 128
SUBLANE = 8
V7X_VMEM_BYTES = 64 * 2**20
VMEM_LIMIT = 56 * 2**20

NEG = -1e30
LOG2E = 1.4426950408889634
INT_MIN = -2**31
INT_MAX = 2**31 - 1

QA = 256
PAD_A = 256
TI = 256
TA = 512
VR = HEAD_DIM + 16
CB = 64
KB = 16
BISECT_UNROLL = 2
SNAP_SPAN = 8.0
BOUND_SLACK = 1.02
SUM_FLOOR = 2.0 ** -60
QB = 128
SB = 1024
PAD_B = LEFT_CHUNKS * CHUNK
WIN_B = PAD_B + QB
UNROLL_B = 8


def _cparams(sem):
    return pltpu.CompilerParams(dimension_semantics=sem, vmem_limit_bytes=VMEM_LIMIT)


def _const_spec(shape):
    nd = len(shape)
    return pl.BlockSpec(shape, lambda *_: (0,) * nd, pipeline_mode=pl.Buffered(1))


def _lo_half(shape):
    return (lax.broadcasted_iota(I32, shape, len(shape) - 1) & HEAD_DIM) == 0


def _dot_t(a, b):
    return lax.dot_general(a, b, (((1,), (1,)), ((), ())), preferred_element_type=F32)


def _dot(a, b):
    return jnp.dot(a, b, preferred_element_type=F32)


def _head_rms(y):
    lo = _lo_half(y.shape)
    z = y * y
    sa = jnp.sum(jnp.where(lo, z, 0.0), axis=-1, keepdims=True)
    sb = jnp.sum(jnp.where(lo, 0.0, z), axis=-1, keepdims=True)
    inv = jnp.where(lo, lax.rsqrt(sa * (1.0 / HEAD_DIM) + EPS), lax.rsqrt(sb * (1.0 / HEAD_DIM) + EPS))
    return y * inv


def _proj_kernel(x_ref, g_ref, *rest, segs, cw):
    ns = len(segs)
    w_refs, s_refs, o_refs = rest[:ns], rest[ns:2 * ns], rest[2 * ns:3 * ns]
    n_scr = rest[3 * ns]
    x = x_ref[...]
    ms = jnp.mean(x * x, axis=-1, keepdims=True)
    n_scr[...] = (x * lax.rsqrt(ms + EPS) * g_ref[...]).astype(n_scr.dtype)
    for (cols, headnorm), w_ref, s_ref, o_ref in zip(segs, w_refs, s_refs, o_refs):
        for c0 in range(0, cols, cw):
            c1 = min(c0 + cw, cols)
            y = _dot(n_scr[...], w_ref[:, c0:c1])
            for l0 in range(0, c1 - c0, LANE):
                yl = y[:, l0:l0 + LANE]
                if headnorm:
                    yl = _head_rms(yl)
                o_ref[:, c0 + l0:c0 + l0 + LANE] = (yl * s_ref[:, c0 + l0:c0 + l0 + LANE]).astype(o_ref.dtype)


def _in_proj(x, g, segs, tm=512, cw=256):
    n, d = x.shape
    ws = [s[0].astype(BF16) for s in segs]
    ss = [s[1].astype(F32).reshape(1, -1) for s in segs]
    meta = tuple((int(s[0].shape[1]), bool(s[2])) for s in segs)
    in_specs = [pl.BlockSpec((tm, d), lambda i: (i, 0)), _const_spec((1, d))]
    in_specs += [_const_spec(w.shape) for w in ws] + [_const_spec(s.shape) for s in ss]
    out_specs = [pl.BlockSpec((tm, c), lambda i: (i, 0)) for c, _ in meta]
    out_shape = [jax.ShapeDtypeStruct((n, c), s[3]) for (c, _), s in zip(meta, segs)]
    return pl.pallas_call(
        functools.partial(_proj_kernel, segs=meta, cw=cw),
        grid=(n // tm,),
        in_specs=in_specs,
        out_specs=out_specs,
        out_shape=out_shape,
        scratch_shapes=[pltpu.VMEM((tm, d), BF16)],
        compiler_params=_cparams(("arbitrary",)),
        name="in_proj",
    )(x, g.astype(F32).reshape(1, d), *ws, *ss)


def _toeplitz_kernel(vec_ref, off_ref, o_ref, *, rows, cols, band):
    n = vec_ref.shape[-1]
    x = jnp.broadcast_to(vec_ref[0], (rows, n))
    t = (pltpu.roll(x, 0, 1, stride=1, stride_axis=0)[:, :cols] - off_ref[0][:, :1]) * LOG2E
    if band is not None:
        r = lax.broadcasted_iota(I32, (rows, cols), 0)
        c = lax.broadcasted_iota(I32, (rows, cols), 1)
        lo = (r // CHUNK) * CHUNK
        t = jnp.where((c >= lo) & (c < lo + band), t, NEG)
    o_ref[0] = t


def _toeplitz(vec, off, rows, cols, band=None):
    h, n = vec.shape
    return pl.pallas_call(
        functools.partial(_toeplitz_kernel, rows=rows, cols=cols, band=band),
        grid=(h,),
        in_specs=[pl.BlockSpec((1, 1, n), lambda i: (i, 0, 0)), pl.BlockSpec((1, 1, LANE), lambda i: (i, 0, 0))],
        out_specs=pl.BlockSpec((1, rows, cols), lambda i: (i, 0, 0)),
        out_shape=jax.ShapeDtypeStruct((h, rows, cols), F32),
        compiler_params=_cparams(("arbitrary",)),
        name="toeplitz_bias",
    )(vec.reshape(h, 1, n), jnp.broadcast_to(off.reshape(h, 1, 1), (h, 1, LANE)))


def _t5_bucket_static(rel):
    n = np.abs(rel)
    large = 8 + sum((n >= t).astype(np.int64) for t in (12, 16, 23, 32, 46, 64, 91))
    return np.where(rel > 0, 16, 0) + np.where(n < 8, n, large)


def _bias_a(t5_table):
    n = 2 * TA
    j = np.arange(n)
    d = np.where(j < n // 2, j, j - n)
    bucket = _t5_bucket_static(-d - PAD_A)
    vec = t5_table[bucket, :].T.astype(F32)
    far = t5_table[T5_BUCKETS // 2 - 1, :].astype(F32)
    bias_max = jnp.maximum(jnp.max(t5_table.astype(F32) - far[None, :]), 0.0) * LOG2E
    return _toeplitz(vec, far, TA, QA), bias_max


def _bias_b(rel_table):
    n = 1024
    j = np.arange(n)
    d = np.where(j < WIN_B, j, j - n)
    idx = np.clip(PAD_B - d, -REL_CLIP, REL_CLIP) + REL_CLIP
    vec = rel_table[idx, :].T.astype(F32)
    return _toeplitz(vec, jnp.zeros((vec.shape[0],), F32), QB, WIN_B, band=PAD_B + CHUNK)


def _attn_b_kernel(q_ref, k_ref, v_ref, b_ref, o_ref):
    j = pl.program_id(2)
    lo = _lo_half((QB, LANE))
    bias = b_ref[...].reshape(2 * QB, WIN_B)
    col = lax.broadcasted_iota(I32, (2 * QB, WIN_B), 1)

    def block(t):
        q0 = pl.multiple_of(j * SB + t * QB, QB)
        qs = q_ref[0, pl.ds(pl.multiple_of(t * QB, QB), QB), :]
        zero = jnp.zeros_like(qs)
        qq = jnp.concatenate([jnp.where(lo, qs, zero), jnp.where(lo, zero, qs)], axis=0)
        s = _dot_t(qq, k_ref[0, pl.ds(q0, WIN_B), :]) + bias
        s = jnp.where(col >= PAD_B - q0, s, NEG)
        m = jnp.max(s, axis=-1, keepdims=True)
        p = jnp.exp2(s - m)
        l = jnp.sum(p, axis=-1, keepdims=True)
        pv = _dot(p.astype(v_ref.dtype), v_ref[0, pl.ds(q0, WIN_B), :]) / l
        o_ref[0, pl.ds(pl.multiple_of(t * QB, QB), QB), :] = jnp.where(lo, pv[:QB], pv[QB:]).astype(o_ref.dtype)

    def body(t2, carry):
        for k in range(UNROLL_B):
            block(t2 * UNROLL_B + k)
        return carry

    lax.fori_loop(0, SB // QB // UNROLL_B, body, 0)


def _attn_b(q, kp, vp, bias):
    b, s, d = q.shape
    npair = d // LANE
    return pl.pallas_call(
        _attn_b_kernel,
        grid=(b, npair, s // SB),
        in_specs=[
            pl.BlockSpec((1, SB, LANE), lambda bi, p, j: (bi, j, p)),
            pl.BlockSpec((1, s + PAD_B, LANE), lambda bi, p, j: (bi, 0, p)),
            pl.BlockSpec((1, s + PAD_B, LANE), lambda bi, p, j: (bi, 0, p)),
            pl.BlockSpec((2, QB, WIN_B), lambda bi, p, j: (p, 0, 0)),
        ],
        out_specs=pl.BlockSpec((1, SB, LANE), lambda bi, p, j: (bi, j, p)),
        out_shape=jax.ShapeDtypeStruct((b, s, d), q.dtype),
        compiler_params=_cparams(("arbitrary", "arbitrary", "arbitrary")),
        name="attn_band",
    )(q, kp, vp, bias)


KEY_FINITE_MAX = 0x7F7FFFFF
KEY_FINITE_MIN = -0x7F800000


def _score_to_key(x):
    bits = lax.bitcast_convert_type(x, I32)
    return bits ^ ((bits >> 31) & 0x7FFFFFFF)


def _key_to_score(k):
    k = jnp.minimum(jnp.maximum(k, KEY_FINITE_MIN), KEY_FINITE_MAX)
    return lax.bitcast_convert_type(k ^ ((k >> 31) & 0x7FFFFFFF), F32)


def _count(keys, start, ntile, pred):
    nacc = 4

    def tile(t, accs):
        accs = list(accs)
        base = start + t * TA
        for cb in range(TA // CB):
            blk = keys[pl.ds(base + cb * CB, CB), :]
            for c in range(CB // SUBLANE):
                kk = blk[c * SUBLANE:(c + 1) * SUBLANE]
                a = accs[c % nacc]
                accs[c % nacc] = jnp.where(pred(kk, base + cb * CB + c * SUBLANE), a + 1, a)
        return tuple(accs)

    accs = lax.fori_loop(0, ntile, tile, tuple(jnp.zeros((SUBLANE, QA), I32) for _ in range(nacc)))
    tot = (accs[0] + accs[1]) + (accs[2] + accs[3])
    return jnp.sum(tot, axis=0, keepdims=True).astype(F32)


def _snap(keys, start, ntile, lo, hi):
    def tile(t, accs):
        mn, mx = list(accs[:2]), list(accs[2:])
        base = start + t * TA
        for cb in range(TA // CB):
            blk = keys[pl.ds(base + cb * CB, CB), :]
            for c in range(CB // SUBLANE):
                kk = blk[c * SUBLANE:(c + 1) * SUBLANE]
                mn[c % 2] = jnp.minimum(mn[c % 2], jnp.where(kk >= lo, kk, INT_MAX))
                mx[c % 2] = jnp.maximum(mx[c % 2], jnp.where(kk < hi, kk, INT_MIN))
        return tuple(mn + mx)

    top = jnp.full((SUBLANE, QA), INT_MAX, I32)
    bot = jnp.full((SUBLANE, QA), INT_MIN, I32)
    mn0, mn1, mx0, mx1 = lax.fori_loop(0, ntile, tile, (top, top, bot, bot))
    return (jnp.min(jnp.minimum(mn0, mn1), axis=0, keepdims=True),
            jnp.max(jnp.maximum(mx0, mx1), axis=0, keepdims=True))


def _demote_ties(keys, start, ntile, thr, keep, tie):
    rid = lax.broadcasted_iota(I32, (SUBLANE, QA), 0)

    def block(c, seen):
        r0 = start + c * CB
        blk = keys[pl.ds(r0, CB), :]
        out = []
        for s in range(CB // SUBLANE):
            kk = blk[s * SUBLANE:(s + 1) * SUBLANE]
            eq = (kk == thr) & tie
            e = jnp.where(eq, 1, 0)
            for sh in (1, 2, 4):
                e = e + jnp.where(rid >= sh, pltpu.roll(e, sh, 0), 0)
            out.append(jnp.where(eq & (seen + e > keep), INT_MIN, kk))
            seen = seen + e[SUBLANE - 1:SUBLANE]
        keys[pl.ds(r0, CB), :] = jnp.concatenate(out, axis=0)
        return seen

    lax.fori_loop(0, ntile * (TA // CB), block, jnp.zeros((1, QA), I32))


def _attn_a_kernel(qt_ref, qit_ref, wit_ref, k_ref, ki_ref, vt_ref, b_ref, top_ref, o_ref,
                   keys, x0_scr, x1_scr, s_scr, p_scr, qs_scr, qis_scr, acc_scr, m_scr, al_scr, mb_scr, gm_scr,
                   lo_scr, hi_scr, cl_scr, ch_scr):
    i = pl.program_id(1)
    topk = float(TOPK_MAX)
    sub_lo = lax.broadcasted_iota(I32, (LANE, QA), 0) < HEAD_DIM

    for t in range(8):
        qt = qt_ref[0, t * LANE:(t + 1) * LANE, :]
        gp, r = divmod(t, 4)
        zero = jnp.zeros_like(qt)
        qs_scr[2 * gp, :, r * QA:(r + 1) * QA] = jnp.where(sub_lo, qt, zero)
        qs_scr[2 * gp + 1, :, r * QA:(r + 1) * QA] = jnp.where(sub_lo, zero, qt)
    for p in range(IDX_HEADS // 2):
        qt = qit_ref[0, p * LANE:(p + 1) * LANE, :]
        zero = jnp.zeros_like(qt)
        qis_scr[:, (2 * p) * QA:(2 * p + 1) * QA] = jnp.where(sub_lo, qt, zero)
        qis_scr[:, (2 * p + 1) * QA:(2 * p + 2) * QA] = jnp.where(sub_lo, zero, qt)

    gm_scr[...] = jnp.full((TI, QA), INT_MIN, I32)
    qcol = lax.broadcasted_iota(I32, (CB, QA), 1)
    krow = lax.broadcasted_iota(I32, (CB, QA), 0)
    last = i + 1
    x_scr = (x0_scr, x1_scr)

    def score_dot(tt, buf):
        x_scr[buf][...] = _dot(ki_ref[0, pl.ds(pl.multiple_of(tt * TI, TI), TI), :], qis_scr[...])

    def score_keys(tt, buf):
        base = pl.multiple_of(tt * TI, TI)
        for cb in range(TI // CB):
            sc = jnp.zeros((CB, QA), F32)
            for h in range(IDX_HEADS):
                sc = sc + wit_ref[0, h:h + 1, :] * jnp.maximum(x_scr[buf][cb * CB:(cb + 1) * CB, h * QA:(h + 1) * QA], 0.0)
            causal = (krow + cb * CB < (qcol // CHUNK + 1) * CHUNK) | (tt < last)
            key = jnp.where(causal & (tt > 0), _score_to_key(sc), INT_MIN)
            keys[pl.ds(base + cb * CB, CB), :] = key
            gm_scr[cb * CB:(cb + 1) * CB, :] = jnp.maximum(gm_scr[cb * CB:(cb + 1) * CB, :], key)

    first = i & 1
    score_dot(first, 0)

    def score_pair(p, carry):
        ta = first + 2 * p
        score_dot(ta + 1, 1)
        score_keys(ta, 0)
        score_dot(jnp.minimum(ta + 2, last), 0)
        score_keys(ta + 1, 1)
        return carry

    lax.fori_loop(0, (i + 2 - first) // 2, score_pair, 0)

    gm = gm_scr[...]
    s_start = pl.multiple_of((i & 1) * PAD_A, PAD_A)
    s_ntile = (i + 2) // 2
    lo0 = jnp.min(gm, axis=0, keepdims=True)
    lo_scr[...] = lo0
    hi_scr[...] = jnp.max(gm, axis=0, keepdims=True) + 1
    cl_scr[...] = _count(keys, s_start, s_ntile, lambda kk, _: kk >= lo0)
    ch_scr[...] = jnp.zeros((1, QA), F32)

    def open_span():
        opn = hi_scr[...] - 1 > lo_scr[...]
        return jnp.max(jnp.where(opn, jnp.maximum(cl_scr[...] - ch_scr[...], 1.0), 0.0))

    def search(state):
        span, prev = state

        @pl.when((span <= SNAP_SPAN) | (span >= prev))
        def _():
            lo, hi = lo_scr[...], hi_scr[...]
            amin, amax = _snap(keys, s_start, s_ntile, lo, hi)
            opn = hi - 1 > lo
            lo_scr[...] = jnp.where(opn, amin, lo)
            hi_scr[...] = jnp.where(opn, amax + 1, hi)

        for by_value in (True, False):
            lo, hi = lo_scr[...], hi_scr[...]
            cl, ch = cl_scr[...], ch_scr[...]
            if by_value:
                fm = 0.5 * _key_to_score(lo) + 0.5 * _key_to_score(hi)
                mid = jnp.where(hi - 1 > lo, jnp.minimum(jnp.maximum(_score_to_key(fm), lo + 1), hi - 1), lo)
            else:
                mid = (lo >> 1) + (hi >> 1) + (lo & hi & 1)
            cnt = _count(keys, s_start, s_ntile, lambda kk, _: kk >= mid)
            ge = cnt >= topk
            exact = cnt == topk
            lo_scr[...] = jnp.where(ge, mid, lo)
            cl_scr[...] = jnp.where(ge, cnt, cl)
            hi_scr[...] = jnp.where(exact, mid + 1, jnp.where(ge, hi, mid))
            ch_scr[...] = jnp.where(ge, ch, cnt)
        return open_span(), span

    lax.while_loop(lambda st: st[0] > 0.0, search, (open_span(), jnp.float32(3e38)))

    thr = lo_scr[...]
    tie = (cl_scr[...] > topk) & (thr > INT_MIN)

    @pl.when(jnp.max(jnp.where(tie, 1.0, 0.0)) > 0.0)
    def _():
        _demote_ties(keys, s_start, s_ntile, thr, (topk - ch_scr[...]).astype(I32), tie)

    thr = jnp.maximum(thr, INT_MIN + 1)

    top = top_ref[0:1, 0:1]

    def attend(start, with_bias, running_max):
        shift = 0.0 if running_max else -top
        for cb in range(TA // CB):
            rows = pl.ds(start + cb * CB, CB)
            mb_scr[cb * CB:(cb + 1) * CB, :] = jnp.where(keys[rows, :] >= thr, shift, NEG)
        blk = start // TI

        def qk(g):
            s_scr[g % 2] = _dot(k_ref[0, pl.ds(start, TA), (g // 2) * LANE:(g // 2 + 1) * LANE], qs_scr[g])

        qk(0)
        for g in range(KV_GROUPS):
            buf = g % 2
            if g + 1 < KV_GROUPS:
                qk(g + 1)
            hds = [g * Q_PER_KV + r for r in range(Q_PER_KV)]

            def logits(kb):
                rows = slice(kb * KB, (kb + 1) * KB)
                mb = mb_scr[rows, :]
                out = []
                for r in range(Q_PER_KV):
                    s = s_scr[buf, rows, r * QA:(r + 1) * QA] + mb
                    if with_bias:
                        s = s + b_ref[hds[r], rows, :]
                    out.append(s)
                return out

            if running_max:
                mx = logits(0)
                for kb in range(1, TA // KB):
                    mx = [jnp.maximum(a, s) for a, s in zip(mx, logits(kb))]
                m_new = []
                for r in range(Q_PER_KV):
                    m_old = m_scr[hds[r]]
                    mn = jnp.maximum(m_old, jnp.max(mx[r], axis=0, keepdims=True))
                    al_scr[hds[r]] = jnp.exp2(m_old - mn)
                    m_scr[hds[r]] = mn
                    m_new.append(jnp.broadcast_to(mn, (KB, QA)))
            for kb in range(TA // KB):
                ss = logits(kb)
                for r in range(Q_PER_KV):
                    t = ss[r] - m_new[r] if running_max else ss[r]
                    p_scr[buf, kb * KB:(kb + 1) * KB, r * QA:(r + 1) * QA] = jnp.exp2(t).astype(p_scr.dtype)
            for r in range(Q_PER_KV):
                cols = slice(r * QA, (r + 1) * QA)
                pv = _dot(vt_ref[0, g, blk], p_scr[buf, 0:TI, cols]) + _dot(vt_ref[0, g, blk + 1], p_scr[buf, TI:TA, cols])
                if running_max:
                    acc_scr[hds[r]] = al_scr[hds[r]] * acc_scr[hds[r]] + pv
                else:
                    acc_scr[hds[r]] = acc_scr[hds[r]] + pv

    def sweep(running_max):
        acc_scr[...] = jnp.zeros(acc_scr.shape, F32)

        def far_body(t, carry):
            attend(pl.multiple_of((i & 1) * PAD_A + t * TA, PAD_A), False, running_max)
            return carry

        lax.fori_loop(0, i // 2, far_body, 0)
        attend(pl.multiple_of(i * QA, QA), True, running_max)

    sweep(False)
    lmin = acc_scr[0, HEAD_DIM:HEAD_DIM + 1, :]
    for hd in range(1, acc_scr.shape[0]):
        lmin = jnp.minimum(lmin, acc_scr[hd, HEAD_DIM:HEAD_DIM + 1, :])

    @pl.when(jnp.min(lmin) < SUM_FLOOR)
    def _():
        m_scr[...] = jnp.full(m_scr.shape, NEG, F32)
        sweep(True)

    for t in range(8):
        gp, r = divmod(t, 4)
        aa = acc_scr[(2 * gp) * Q_PER_KV + r]
        ab = acc_scr[(2 * gp + 1) * Q_PER_KV + r]
        oa = aa[:HEAD_DIM] / aa[HEAD_DIM:HEAD_DIM + 1]
        ob = ab[:HEAD_DIM] / ab[HEAD_DIM:HEAD_DIM + 1]
        o_ref[0, :, t * LANE:(t + 1) * LANE] = jnp.concatenate([oa, ob], axis=0).T.astype(o_ref.dtype)


def _attn_a(qt, qit, wit, kp, kip, vt, bias, top):
    b, d, s = qt.shape
    sp = s + PAD_A
    nh = d // HEAD_DIM
    return pl.pallas_call(
        _attn_a_kernel,
        grid=(b, s // QA),
        in_specs=[
            pl.BlockSpec((1, d, QA), lambda bi, i: (bi, 0, i)),
            pl.BlockSpec((1, qit.shape[1], QA), lambda bi, i: (bi, 0, i)),
            pl.BlockSpec((1, IDX_HEADS, QA), lambda bi, i: (bi, 0, i)),
            pl.BlockSpec((1, sp, kp.shape[-1]), lambda bi, i: (bi, 0, 0), pipeline_mode=pl.Buffered(1)),
            pl.BlockSpec((1, sp, LANE), lambda bi, i: (bi, 0, 0), pipeline_mode=pl.Buffered(1)),
            pl.BlockSpec((1,) + vt.shape[1:], lambda bi, i: (bi, 0, 0, 0, 0), pipeline_mode=pl.Buffered(1)),
            _const_spec(bias.shape),
            _const_spec(top.shape),
        ],
        out_specs=pl.BlockSpec((1, QA, d), lambda bi, i: (bi, i, 0)),
        out_shape=jax.ShapeDtypeStruct((b, s, d), qt.dtype),
        scratch_shapes=[
            pltpu.VMEM((sp, QA), I32),
            pltpu.VMEM((TI, IDX_HEADS * QA), F32),
            pltpu.VMEM((TI, IDX_HEADS * QA), F32),
            pltpu.VMEM((2, TA, Q_PER_KV * QA), F32),
            pltpu.VMEM((2, TA, Q_PER_KV * QA), qt.dtype),
            pltpu.VMEM((KV_GROUPS, LANE, Q_PER_KV * QA), qt.dtype),
            pltpu.VMEM((LANE, IDX_HEADS * QA), qit.dtype),
            pltpu.VMEM((nh, VR, QA), F32),
            pltpu.VMEM((nh, 1, QA), F32),
            pltpu.VMEM((nh, 1, QA), F32),
            pltpu.VMEM((TA, QA), F32),
            pltpu.VMEM((TI, QA), I32),
            pltpu.VMEM((1, QA), I32),
            pltpu.VMEM((1, QA), I32),
            pltpu.VMEM((1, QA), F32),
            pltpu.VMEM((1, QA), F32),
        ],
        compiler_params=_cparams(("arbitrary", "arbitrary")),
        name="attn_sparse",
    )(qt, qit, wit, kp, kip, vt, bias, top)


HALO = 16


def _ffn_kernel(h_ref, hh_ref, m_ref, mh_ref, wo_ref, g_ref, wup_ref, cw_ref, cb_ref, wdn_ref, o_ref,
                me_scr, n_scr, h1_scr, u_scr, acc_scr, *, tm, dff, cw, seq):
    i = pl.program_id(0)
    first = (i * tm) % seq == 0
    me_scr[0:HALO] = mh_ref[...]
    me_scr[HALO:] = m_ref[...]
    h1_scr[...] = _dot(me_scr[...], wo_ref[...])
    h1_scr[0:HALO] = h1_scr[0:HALO] + hh_ref[...]
    h1_scr[HALO:] = h1_scr[HALO:] + h_ref[...]
    h1 = h1_scr[...]
    ms = jnp.mean(h1 * h1, axis=-1, keepdims=True)
    n = h1 * lax.rsqrt(ms + EPS) * g_ref[...]
    row = lax.broadcasted_iota(I32, n.shape, 0)
    n_scr[...] = jnp.where((row < HALO) & first, 0.0, n).astype(n_scr.dtype)
    acc_scr[...] = jnp.zeros(acc_scr.shape, F32)

    def up(c):
        for part in range(2):
            c0 = part * dff + c * cw
            u_scr[c % 2, part] = _dot(n_scr[...], wup_ref[:, c0:c0 + cw])

    up(0)
    for c in range(dff // cw):
        if c + 1 < dff // cw:
            up(c + 1)
        ys = []
        for part in range(2):
            c0 = part * dff + c * cw
            u = u_scr.at[c % 2, part]
            y = cb_ref[:, c0:c0 + cw] + cw_ref[0:1, c0:c0 + cw] * u[HALO - 2:HALO - 2 + tm, :]
            y = y + cw_ref[1:2, c0:c0 + cw] * u[HALO - 1:HALO - 1 + tm, :]
            y = y + cw_ref[2:3, c0:c0 + cw] * u[HALO:HALO + tm, :]
            ys.append(y)
        a, gte = ys
        act = (gte * (1.0 / (1.0 + jnp.exp(-gte)))) * a
        acc_scr[...] += _dot(act.astype(n_scr.dtype), wdn_ref[c * cw:(c + 1) * cw, :])
    o_ref[...] = h1_scr[HALO:] + acc_scr[...]


def _out_ffn(h, m, w_out, g, w_up, conv_w, conv_b, w_down, seq, tm=512, cw=256):
    n, d = h.shape
    dff = w_down.shape[0]
    hb = tm // HALO
    halo = lambda i: (jnp.maximum(i * hb - 1, 0), 0)
    return pl.pallas_call(
        functools.partial(_ffn_kernel, tm=tm, dff=dff, cw=cw, seq=seq),
        grid=(n // tm,),
        in_specs=[
            pl.BlockSpec((tm, d), lambda i: (i, 0)),
            pl.BlockSpec((HALO, d), halo),
            pl.BlockSpec((tm, d), lambda i: (i, 0)),
            pl.BlockSpec((HALO, d), halo),
            _const_spec((d, d)),
            _const_spec((1, d)),
            _const_spec((d, 2 * dff)),
            _const_spec((3, 2 * dff)),
            _const_spec((1, 2 * dff)),
            _const_spec((dff, d)),
        ],
        out_specs=pl.BlockSpec((tm, d), lambda i: (i, 0)),
        out_shape=jax.ShapeDtypeStruct((n, d), F32),
        scratch_shapes=[
            pltpu.VMEM((tm + HALO, d), BF16),
            pltpu.VMEM((tm + HALO, d), BF16),
            pltpu.VMEM((tm + HALO, d), F32),
            pltpu.VMEM((2, 2, tm + HALO, cw), F32),
            pltpu.VMEM((tm, d), F32),
        ],
        compiler_params=_cparams(("arbitrary",)),
        name="out_ffn",
    )(h, h, m, m, w_out.astype(BF16), g.astype(F32).reshape(1, d), w_up.astype(BF16),
      conv_w.astype(F32), conv_b.astype(F32).reshape(1, -1), w_down.astype(BF16))


_HEAD_PERM = np.array([8 * gp + 4 * half + r for gp in range(2) for r in range(4) for half in range(2)])
_COL_PERM = (_HEAD_PERM[:, None] * HEAD_DIM + np.arange(HEAD_DIM)[None, :]).reshape(-1)


def _pad_front(x, b, s, pad):
    return jnp.pad(x.reshape(b, s, x.shape[-1]), ((0, 0), (pad, 0), (0, 0)))


def _mixer_a(h2, b, s, g_attn, w_in, q_g, k_g, bias, bias_max):
    ad = Q_PER_KV * KV_GROUPS * HEAD_DIM
    kd = KV_GROUPS * HEAD_DIM
    o_q, o_k, o_v, o_qi = 0, ad, ad + kd, ad + 2 * kd
    o_ki = o_qi + IDX_HEADS * IDX_DIM
    o_wi = o_ki + IDX_DIM
    w_ki = w_in[:, o_ki:o_wi]
    w_wi = jnp.pad(w_in[:, o_wi:o_wi + IDX_HEADS], ((0, 0), (0, LANE - IDX_HEADS)))
    wi_scale = IDX_HEADS ** -0.5 * IDX_DIM ** -0.5
    ones = lambda c: jnp.ones((c,), F32)
    segs = [
        (w_in[:, o_q:o_k][:, _COL_PERM], jnp.tile(q_g, ad // HEAD_DIM) * (HEAD_DIM ** -0.5 * LOG2E), True, BF16),
        (w_in[:, o_k:o_v], jnp.tile(k_g, KV_GROUPS), True, BF16),
        (w_in[:, o_v:o_qi], ones(kd), False, BF16),
        (w_in[:, o_qi:o_ki], ones(IDX_HEADS * IDX_DIM), False, BF16),
        (jnp.concatenate([w_ki, w_ki], axis=1), ones(LANE), True, BF16),
        (w_wi, ones(LANE) * wi_scale, False, F32),
    ]
    q, k, v, qi, ki, wi = _in_proj(h2, g_attn, segs)
    tr = lambda x: x.reshape(b, s, x.shape[-1]).transpose(0, 2, 1)
    sp = s + PAD_A
    vt = jnp.pad(tr(v), ((0, 0), (0, 0), (PAD_A, 0))).reshape(b, KV_GROUPS, HEAD_DIM, sp)
    vt = jnp.concatenate([vt, jnp.ones((b, KV_GROUPS, VR - HEAD_DIM, sp), vt.dtype)], axis=2)
    vt = vt.reshape(b, KV_GROUPS, VR, sp // TI, TI).transpose(0, 1, 3, 2, 4)
    top = (HEAD_DIM ** 0.5 * LOG2E * BOUND_SLACK) * jnp.max(jnp.abs(q_g)) * jnp.max(jnp.abs(k_g)) + bias_max
    o = _attn_a(tr(q), tr(qi), tr(wi[:, :IDX_HEADS]), _pad_front(k, b, s, PAD_A), _pad_front(ki, b, s, PAD_A),
                vt, bias, jnp.broadcast_to(top.astype(F32), (1, LANE)))
    return o.reshape(b * s, ad)


def _mixer_b(h2, b, s, g_attn, w_in, q_g, k_g, bias):
    d = w_in.shape[1] // 3
    nh = d // HEAD_DIM
    segs = [
        (w_in[:, :d], jnp.tile(q_g, nh) * (HEAD_DIM ** -0.5 * LOG2E), True, BF16),
        (w_in[:, d:2 * d], jnp.tile(k_g, nh), True, BF16),
        (w_in[:, 2 * d:], jnp.ones((d,), F32), False, BF16),
    ]
    q, k, v = _in_proj(h2, g_attn, segs)
    o = _attn_b(q.reshape(b, s, d), _pad_front(k, b, s, PAD_B), _pad_front(v, b, s, PAD_B), bias)
    return o.reshape(b * s, d)


def kernel(x, attn_norm_g, w_in_a, w_in_b, q_norm_g, k_norm_g, t5_bias, rel_bias_b, w_out, ffn_norm_g, w_up, conv_w, conv_b, w_down):
    b, s, d = x.shape
    depth = attn_norm_g.shape[0]
    assert s % SB == 0 and s % QA == 0 and min(TOPK_MAX, s // 4) == TOPK_MAX
    h = x.reshape(b * s, d)
    bias_a, bias_a_max = _bias_a(t5_bias)
    for i in range(depth):
        if i % 2 == 0:
            m = _mixer_a(h, b, s, attn_norm_g[i], w_in_a[i // 2], q_norm_g[i], k_norm_g[i], bias_a, bias_a_max)
            wo = w_out[i][_COL_PERM, :]
        else:
            m = _mixer_b(h, b, s, attn_norm_g[i], w_in_b[i // 2], q_norm_g[i], k_norm_g[i], _bias_b(rel_bias_b[i // 2]))
            wo = w_out[i]
        h = _out_ffn(h, m, wo, ffn_norm_g[i], w_up[i], conv_w[i], conv_b[i], w_down[i], s)
    return h.reshape(b, s, d)
```

```python
import functools

import numpy as np
import jax
import jax.numpy as jnp
from jax import lax
from jax.experimental import pallas as pl
from jax.experimental.pallas import tpu as pltpu

F32 = jnp.float32
BF16 = jnp.bfloat16
I32 = jnp.int32

EPS = 1e-6
CHUNK = 64
HEAD_DIM = 64
KV_GROUPS = 4
Q_PER_KV = 4
IDX_HEADS = 8
IDX_DIM = 64
TOPK_MAX = 256
T5_BUCKETS = 32
LEFT_CHUNKS = 8
REL_CLIP = 256

LANE = 128
SUBLANE = 8
V7X_VMEM_BYTES = 64 * 2**20
VMEM_LIMIT = 56 * 2**20

NEG = -1e30
LOG2E = 1.4426950408889634
INT_MIN = -2**31
INT_MAX = 2**31 - 1

QA = 256
PAD_A = 256
TI = 256
TA = 512
VR = HEAD_DIM + 16
CB = 64
KB = 16
BISECT_UNROLL = 2
SNAP_SPAN = 8.0
BOUND_SLACK = 1.02
SUM_FLOOR = 2.0 ** -60
QB = 128
SB = 1024
PAD_B = LEFT_CHUNKS * CHUNK
WIN_B = PAD_B + QB
UNROLL_B = 8


def _cparams(sem):
    return pltpu.CompilerParams(dimension_semantics=sem, vmem_limit_bytes=VMEM_LIMIT)


def _const_spec(shape):
    nd = len(shape)
    return pl.BlockSpec(shape, lambda *_: (0,) * nd, pipeline_mode=pl.Buffered(1))


def _lo_half(shape):
    return (lax.broadcasted_iota(I32, shape, len(shape) - 1) & HEAD_DIM) == 0


def _dot_t(a, b):
    return lax.dot_general(a, b, (((1,), (1,)), ((), ())), preferred_element_type=F32)


def _dot(a, b):
    return jnp.dot(a, b, preferred_element_type=F32)


def _head_rms(y):
    lo = _lo_half(y.shape)
    z = y * y
    sa = jnp.sum(jnp.where(lo, z, 0.0), axis=-1, keepdims=True)
    sb = jnp.sum(jnp.where(lo, 0.0, z), axis=-1, keepdims=True)
    inv = jnp.where(lo, lax.rsqrt(sa * (1.0 / HEAD_DIM) + EPS), lax.rsqrt(sb * (1.0 / HEAD_DIM) + EPS))
    return y * inv


def _proj_kernel(x_ref, g_ref, *rest, segs, cw):
    ns = len(segs)
    w_refs, s_refs, o_refs = rest[:ns], rest[ns:2 * ns], rest[2 * ns:3 * ns]
    n_scr = rest[3 * ns]
    x = x_ref[...]
    ms = jnp.mean(x * x, axis=-1, keepdims=True)
    n_scr[...] = (x * lax.rsqrt(ms + EPS) * g_ref[...]).astype(n_scr.dtype)
    for (cols, headnorm), w_ref, s_ref, o_ref in zip(segs, w_refs, s_refs, o_refs):
        for c0 in range(0, cols, cw):
            c1 = min(c0 + cw, cols)
            y = _dot(n_scr[...], w_ref[:, c0:c1])
            for l0 in range(0, c1 - c0, LANE):
                yl = y[:, l0:l0 + LANE]
                if headnorm:
                    yl = _head_rms(yl)
                o_ref[:, c0 + l0:c0 + l0 + LANE] = (yl * s_ref[:, c0 + l0:c0 + l0 + LANE]).astype(o_ref.dtype)


def _in_proj(x, g, segs, tm=512, cw=256):
    n, d = x.shape
    ws = [s[0].astype(BF16) for s in segs]
    ss = [s[1].astype(F32).reshape(1, -1) for s in segs]
    meta = tuple((int(s[0].shape[1]), bool(s[2])) for s in segs)
    in_specs = [pl.BlockSpec((tm, d), lambda i: (i, 0)), _const_spec((1, d))]
    in_specs += [_const_spec(w.shape) for w in ws] + [_const_spec(s.shape) for s in ss]
    out_specs = [pl.BlockSpec((tm, c), lambda i: (i, 0)) for c, _ in meta]
    out_shape = [jax.ShapeDtypeStruct((n, c), s[3]) for (c, _), s in zip(meta, segs)]
    return pl.pallas_call(
        functools.partial(_proj_kernel, segs=meta, cw=cw),
        grid=(n // tm,),
        in_specs=in_specs,
        out_specs=out_specs,
        out_shape=out_shape,
        scratch_shapes=[pltpu.VMEM((tm, d), BF16)],
        compiler_params=_cparams(("arbitrary",)),
        name="in_proj",
    )(x, g.astype(F32).reshape(1, d), *ws, *ss)


def _toeplitz_kernel(vec_ref, off_ref, o_ref, *, rows, cols, band):
    n = vec_ref.shape[-1]
    x = jnp.broadcast_to(vec_ref[0], (rows, n))
    t = (pltpu.roll(x, 0, 1, stride=1, stride_axis=0)[:, :cols] - off_ref[0][:, :1]) * LOG2E
    if band is not None:
        r = lax.broadcasted_iota(I32, (rows, cols), 0)
        c = lax.broadcasted_iota(I32, (rows, cols), 1)
        lo = (r // CHUNK) * CHUNK
        t = jnp.where((c >= lo) & (c < lo + band), t, NEG)
    o_ref[0] = t


def _toeplitz(vec, off, rows, cols, band=None):
    h, n = vec.shape
    return pl.pallas_call(
        functools.partial(_toeplitz_kernel, rows=rows, cols=cols, band=band),
        grid=(h,),
        in_specs=[pl.BlockSpec((1, 1, n), lambda i: (i, 0, 0)), pl.BlockSpec((1, 1, LANE), lambda i: (i, 0, 0))],
        out_specs=pl.BlockSpec((1, rows, cols), lambda i: (i, 0, 0)),
        out_shape=jax.ShapeDtypeStruct((h, rows, cols), F32),
        compiler_params=_cparams(("arbitrary",)),
        name="toeplitz_bias",
    )(vec.reshape(h, 1, n), jnp.broadcast_to(off.reshape(h, 1, 1), (h, 1, LANE)))


def _t5_bucket_static(rel):
    n = np.abs(rel)
    large = 8 + sum((n >= t).astype(np.int64) for t in (12, 16, 23, 32, 46, 64, 91))
    return np.where(rel > 0, 16, 0) + np.where(n < 8, n, large)


def _bias_a(t5_table):
    n = 2 * TA
    j = np.arange(n)
    d = np.where(j < n // 2, j, j - n)
    bucket = _t5_bucket_static(-d - PAD_A)
    vec = t5_table[bucket, :].T.astype(F32)
    far = t5_table[T5_BUCKETS // 2 - 1, :].astype(F32)
    bias_max = jnp.maximum(jnp.max(t5_table.astype(F32) - far[None, :]), 0.0) * LOG2E
    return _toeplitz(vec, far, TA, QA), bias_max


def _bias_b(rel_table, q_g, k_g):
    n = 1024
    j = np.arange(n)
    d = np.where(j < WIN_B, j, j - n)
    idx = np.clip(PAD_B - d, -REL_CLIP, REL_CLIP) + REL_CLIP
    vec = rel_table[idx, :].T.astype(F32)
    top = (HEAD_DIM ** 0.5 * BOUND_SLACK) * jnp.max(jnp.abs(q_g)) * jnp.max(jnp.abs(k_g)) + jnp.maximum(jnp.max(rel_table), 0.0)
    return _toeplitz(vec, jnp.broadcast_to(top.astype(F32), (vec.shape[0],)), QB, WIN_B, band=PAD_B + CHUNK)


def _attn_b_kernel(q_ref, k_ref, v_ref, b_ref, o_ref):
    j = pl.program_id(2)
    lo = _lo_half((QB, LANE))
    bias = b_ref[...].reshape(2 * QB, WIN_B)
    col = lax.broadcasted_iota(I32, (2 * QB, WIN_B), 1)

    def logits(t):
        q0 = pl.multiple_of(j * SB + t * QB, QB)
        qs = q_ref[0, pl.ds(pl.multiple_of(t * QB, QB), QB), :]
        zero = jnp.zeros_like(qs)
        qq = jnp.concatenate([jnp.where(lo, qs, zero), jnp.where(lo, zero, qs)], axis=0)
        return q0, _dot_t(qq, k_ref[0, pl.ds(q0, WIN_B), :]) + bias

    def store(t, pv):
        o = pv[:, :LANE] / pv[:, LANE:]
        o_ref[0, pl.ds(pl.multiple_of(t * QB, QB), QB), :] = jnp.where(lo, o[:QB], o[QB:]).astype(o_ref.dtype)

    def body(t2, carry):
        sums = None
        for k in range(UNROLL_B):
            t = t2 * UNROLL_B + k
            q0, s = logits(t)
            pv = _dot(jnp.exp2(s).astype(v_ref.dtype), v_ref[0, pl.ds(q0, WIN_B), :])
            store(t, pv)
            sums = pv[:, LANE:] if sums is None else jnp.minimum(sums, pv[:, LANE:])

        @pl.when(jnp.min(sums) < SUM_FLOOR)
        def _():
            for k in range(UNROLL_B):
                t = t2 * UNROLL_B + k
                q0, s = logits(t)
                s = jnp.where(col >= PAD_B - q0, s, NEG)
                p = jnp.exp2(s - jnp.max(s, axis=-1, keepdims=True))
                store(t, _dot(p.astype(v_ref.dtype), v_ref[0, pl.ds(q0, WIN_B), :]))

        return carry

    lax.fori_loop(0, SB // QB // UNROLL_B, body, 0)


def _attn_b(q, kp, vp, bias):
    b, s, d = q.shape
    npair = d // LANE
    return pl.pallas_call(
        _attn_b_kernel,
        grid=(b, npair, s // SB),
        in_specs=[
            pl.BlockSpec((1, SB, LANE), lambda bi, p, j: (bi, j, p)),
            pl.BlockSpec((1, s + PAD_B, LANE), lambda bi, p, j: (bi, 0, p)),
            pl.BlockSpec((1, s + PAD_B, 2 * LANE), lambda bi, p, j: (bi, 0, p)),
            pl.BlockSpec((2, QB, WIN_B), lambda bi, p, j: (p, 0, 0)),
        ],
        out_specs=pl.BlockSpec((1, SB, LANE), lambda bi, p, j: (bi, j, p)),
        out_shape=jax.ShapeDtypeStruct((b, s, d), q.dtype),
        compiler_params=_cparams(("arbitrary", "arbitrary", "arbitrary")),
        name="attn_band",
    )(q, kp, vp, bias)


KEY_FINITE_MAX = 0x7F7FFFFF
KEY_FINITE_MIN = -0x7F800000


def _score_to_key(x):
    bits = lax.bitcast_convert_type(x, I32)
    return bits ^ ((bits >> 31) & 0x7FFFFFFF)


def _key_to_score(k):
    k = jnp.minimum(jnp.maximum(k, KEY_FINITE_MIN), KEY_FINITE_MAX)
    return lax.bitcast_convert_type(k ^ ((k >> 31) & 0x7FFFFFFF), F32)


def _count(keys, start, ntile, pred):
    nacc = 4

    def tile(t, accs):
        accs = list(accs)
        base = start + t * TA
        for cb in range(TA // CB):
            blk = keys[pl.ds(base + cb * CB, CB), :]
            for c in range(CB // SUBLANE):
                kk = blk[c * SUBLANE:(c + 1) * SUBLANE]
                a = accs[c % nacc]
                accs[c % nacc] = jnp.where(pred(kk, base + cb * CB + c * SUBLANE), a + 1, a)
        return tuple(accs)

    accs = lax.fori_loop(0, ntile, tile, tuple(jnp.zeros((SUBLANE, QA), I32) for _ in range(nacc)))
    tot = (accs[0] + accs[1]) + (accs[2] + accs[3])
    return jnp.sum(tot, axis=0, keepdims=True).astype(F32)


def _snap(keys, start, ntile, lo, hi):
    def tile(t, accs):
        mn, mx = list(accs[:2]), list(accs[2:])
        base = start + t * TA
        for cb in range(TA // CB):
            blk = keys[pl.ds(base + cb * CB, CB), :]
            for c in range(CB // SUBLANE):
                kk = blk[c * SUBLANE:(c + 1) * SUBLANE]
                mn[c % 2] = jnp.minimum(mn[c % 2], jnp.where(kk >= lo, kk, INT_MAX))
                mx[c % 2] = jnp.maximum(mx[c % 2], jnp.where(kk < hi, kk, INT_MIN))
        return tuple(mn + mx)

    top = jnp.full((SUBLANE, QA), INT_MAX, I32)
    bot = jnp.full((SUBLANE, QA), INT_MIN, I32)
    mn0, mn1, mx0, mx1 = lax.fori_loop(0, ntile, tile, (top, top, bot, bot))
    return (jnp.min(jnp.minimum(mn0, mn1), axis=0, keepdims=True),
            jnp.max(jnp.maximum(mx0, mx1), axis=0, keepdims=True))


def _demote_ties(keys, start, ntile, thr, keep, tie):
    rid = lax.broadcasted_iota(I32, (SUBLANE, QA), 0)

    def block(c, seen):
        r0 = start + c * CB
        blk = keys[pl.ds(r0, CB), :]
        out = []
        for s in range(CB // SUBLANE):
            kk = blk[s * SUBLANE:(s + 1) * SUBLANE]
            eq = (kk == thr) & tie
            e = jnp.where(eq, 1, 0)
            for sh in (1, 2, 4):
                e = e + jnp.where(rid >= sh, pltpu.roll(e, sh, 0), 0)
            out.append(jnp.where(eq & (seen + e > keep), INT_MIN, kk))
            seen = seen + e[SUBLANE - 1:SUBLANE]
        keys[pl.ds(r0, CB), :] = jnp.concatenate(out, axis=0)
        return seen

    lax.fori_loop(0, ntile * (TA // CB), block, jnp.zeros((1, QA), I32))


def _attn_a_kernel(qt_ref, qit_ref, wit_ref, k_ref, ki_ref, vt_ref, b_ref, top_ref, o_ref,
                   keys, x0_scr, x1_scr, s_scr, p_scr, qs_scr, qis_scr, acc_scr, m_scr, al_scr, mb_scr, gm_scr,
                   lo_scr, hi_scr, cl_scr, ch_scr):
    i = pl.program_id(1)
    topk = float(TOPK_MAX)
    sub_lo = lax.broadcasted_iota(I32, (LANE, QA), 0) < HEAD_DIM

    for t in range(8):
        qt = qt_ref[0, t * LANE:(t + 1) * LANE, :]
        gp, r = divmod(t, 4)
        zero = jnp.zeros_like(qt)
        qs_scr[2 * gp, :, r * QA:(r + 1) * QA] = jnp.where(sub_lo, qt, zero)
        qs_scr[2 * gp + 1, :, r * QA:(r + 1) * QA] = jnp.where(sub_lo, zero, qt)
    for p in range(IDX_HEADS // 2):
        qt = qit_ref[0, p * LANE:(p + 1) * LANE, :]
        zero = jnp.zeros_like(qt)
        qis_scr[:, (2 * p) * QA:(2 * p + 1) * QA] = jnp.where(sub_lo, qt, zero)
        qis_scr[:, (2 * p + 1) * QA:(2 * p + 2) * QA] = jnp.where(sub_lo, zero, qt)

    gm_scr[...] = jnp.full((TI, QA), INT_MIN, I32)
    qcol = lax.broadcasted_iota(I32, (CB, QA), 1)
    krow = lax.broadcasted_iota(I32, (CB, QA), 0)
    last = i + 1
    x_scr = (x0_scr, x1_scr)

    def score_dot(tt, buf):
        x_scr[buf][...] = _dot(ki_ref[0, pl.ds(pl.multiple_of(tt * TI, TI), TI), :], qis_scr[...])

    def score_keys(tt, buf):
        base = pl.multiple_of(tt * TI, TI)
        for cb in range(TI // CB):
            sc = jnp.zeros((CB, QA), F32)
            for h in range(IDX_HEADS):
                sc = sc + wit_ref[0, h:h + 1, :] * jnp.maximum(x_scr[buf][cb * CB:(cb + 1) * CB, h * QA:(h + 1) * QA], 0.0)
            causal = (krow + cb * CB < (qcol // CHUNK + 1) * CHUNK) | (tt < last)
            key = jnp.where(causal & (tt > 0), _score_to_key(sc), INT_MIN)
            keys[pl.ds(base + cb * CB, CB), :] = key
            gm_scr[cb * CB:(cb + 1) * CB, :] = jnp.maximum(gm_scr[cb * CB:(cb + 1) * CB, :], key)

    first = i & 1
    score_dot(first, 0)

    def score_pair(p, carry):
        ta = first + 2 * p
        score_dot(ta + 1, 1)
        score_keys(ta, 0)
        score_dot(jnp.minimum(ta + 2, last), 0)
        score_keys(ta + 1, 1)
        return carry

    lax.fori_loop(0, (i + 2 - first) // 2, score_pair, 0)

    gm = gm_scr[...]
    s_start = pl.multiple_of((i & 1) * PAD_A, PAD_A)
    s_ntile = (i + 2) // 2
    lo0 = jnp.min(gm, axis=0, keepdims=True)
    lo_scr[...] = lo0
    hi_scr[...] = jnp.max(gm, axis=0, keepdims=True) + 1
    cl_scr[...] = _count(keys, s_start, s_ntile, lambda kk, _: kk >= lo0)
    ch_scr[...] = jnp.zeros((1, QA), F32)

    def open_span():
        opn = hi_scr[...] - 1 > lo_scr[...]
        return jnp.max(jnp.where(opn, jnp.maximum(cl_scr[...] - ch_scr[...], 1.0), 0.0))

    def search(state):
        span, prev = state

        @pl.when((span <= SNAP_SPAN) | (span >= prev))
        def _():
            lo, hi = lo_scr[...], hi_scr[...]
            amin, amax = _snap(keys, s_start, s_ntile, lo, hi)
            opn = hi - 1 > lo
            lo_scr[...] = jnp.where(opn, amin, lo)
            hi_scr[...] = jnp.where(opn, amax + 1, hi)

        for by_value in (True, False):
            lo, hi = lo_scr[...], hi_scr[...]
            cl, ch = cl_scr[...], ch_scr[...]
            if by_value:
                fm = 0.5 * _key_to_score(lo) + 0.5 * _key_to_score(hi)
                mid = jnp.where(hi - 1 > lo, jnp.minimum(jnp.maximum(_score_to_key(fm), lo + 1), hi - 1), lo)
            else:
                mid = (lo >> 1) + (hi >> 1) + (lo & hi & 1)
            cnt = _count(keys, s_start, s_ntile, lambda kk, _: kk >= mid)
            ge = cnt >= topk
            exact = cnt == topk
            lo_scr[...] = jnp.where(ge, mid, lo)
            cl_scr[...] = jnp.where(ge, cnt, cl)
            hi_scr[...] = jnp.where(exact, mid + 1, jnp.where(ge, hi, mid))
            ch_scr[...] = jnp.where(ge, ch, cnt)
        return open_span(), span

    lax.while_loop(lambda st: st[0] > 0.0, search, (open_span(), jnp.float32(3e38)))

    thr = lo_scr[...]
    tie = (cl_scr[...] > topk) & (thr > INT_MIN)

    @pl.when(jnp.max(jnp.where(tie, 1.0, 0.0)) > 0.0)
    def _():
        _demote_ties(keys, s_start, s_ntile, thr, (topk - ch_scr[...]).astype(I32), tie)

    thr = jnp.maximum(thr, INT_MIN + 1)

    top = top_ref[0:1, 0:1]

    def attend(start, with_bias, running_max):
        shift = 0.0 if running_max else -top
        for cb in range(TA // CB):
            rows = pl.ds(start + cb * CB, CB)
            mb_scr[cb * CB:(cb + 1) * CB, :] = jnp.where(keys[rows, :] >= thr, shift, NEG)
        blk = start // TI

        def qk(g):
            s_scr[g % 2] = _dot(k_ref[0, pl.ds(start, TA), (g // 2) * LANE:(g // 2 + 1) * LANE], qs_scr[g])

        qk(0)
        for g in range(KV_GROUPS):
            buf = g % 2
            if g + 1 < KV_GROUPS:
                qk(g + 1)
            hds = [g * Q_PER_KV + r for r in range(Q_PER_KV)]

            def logits(kb):
                rows = slice(kb * KB, (kb + 1) * KB)
                mb = mb_scr[rows, :]
                out = []
                for r in range(Q_PER_KV):
                    s = s_scr[buf, rows, r * QA:(r + 1) * QA] + mb
                    if with_bias:
                        s = s + b_ref[hds[r], rows, :]
                    out.append(s)
                return out

            if running_max:
                mx = logits(0)
                for kb in range(1, TA // KB):
                    mx = [jnp.maximum(a, s) for a, s in zip(mx, logits(kb))]
                m_new = []
                for r in range(Q_PER_KV):
                    m_old = m_scr[hds[r]]
                    mn = jnp.maximum(m_old, jnp.max(mx[r], axis=0, keepdims=True))
                    al_scr[hds[r]] = jnp.exp2(m_old - mn)
                    m_scr[hds[r]] = mn
                    m_new.append(jnp.broadcast_to(mn, (KB, QA)))
            for kb in range(TA // KB):
                ss = logits(kb)
                for r in range(Q_PER_KV):
                    t = ss[r] - m_new[r] if running_max else ss[r]
                    p_scr[buf, kb * KB:(kb + 1) * KB, r * QA:(r + 1) * QA] = jnp.exp2(t).astype(p_scr.dtype)
            for r in range(Q_PER_KV):
                cols = slice(r * QA, (r + 1) * QA)
                pv = _dot(vt_ref[0, g, blk], p_scr[buf, 0:TI, cols]) + _dot(vt_ref[0, g, blk + 1], p_scr[buf, TI:TA, cols])
                if running_max:
                    acc_scr[hds[r]] = al_scr[hds[r]] * acc_scr[hds[r]] + pv
                else:
                    acc_scr[hds[r]] = acc_scr[hds[r]] + pv

    def sweep(running_max):
        acc_scr[...] = jnp.zeros(acc_scr.shape, F32)

        def far_body(t, carry):
            attend(pl.multiple_of((i & 1) * PAD_A + t * TA, PAD_A), False, running_max)
            return carry

        lax.fori_loop(0, i // 2, far_body, 0)
        attend(pl.multiple_of(i * QA, QA), True, running_max)

    sweep(False)
    lmin = acc_scr[0, HEAD_DIM:HEAD_DIM + 1, :]
    for hd in range(1, acc_scr.shape[0]):
        lmin = jnp.minimum(lmin, acc_scr[hd, HEAD_DIM:HEAD_DIM + 1, :])

    @pl.when(jnp.min(lmin) < SUM_FLOOR)
    def _():
        m_scr[...] = jnp.full(m_scr.shape, NEG, F32)
        sweep(True)

    for t in range(8):
        gp, r = divmod(t, 4)
        aa = acc_scr[(2 * gp) * Q_PER_KV + r]
        ab = acc_scr[(2 * gp + 1) * Q_PER_KV + r]
        oa = aa[:HEAD_DIM] / aa[HEAD_DIM:HEAD_DIM + 1]
        ob = ab[:HEAD_DIM] / ab[HEAD_DIM:HEAD_DIM + 1]
        o_ref[0, :, t * LANE:(t + 1) * LANE] = jnp.concatenate([oa, ob], axis=0).T.astype(o_ref.dtype)


def _attn_a(qt, qit, wit, kp, kip, vt, bias, top):
    b, d, s = qt.shape
    sp = s + PAD_A
    nh = d // HEAD_DIM
    return pl.pallas_call(
        _attn_a_kernel,
        grid=(b, s // QA),
        in_specs=[
            pl.BlockSpec((1, d, QA), lambda bi, i: (bi, 0, i)),
            pl.BlockSpec((1, qit.shape[1], QA), lambda bi, i: (bi, 0, i)),
            pl.BlockSpec((1, IDX_HEADS, QA), lambda bi, i: (bi, 0, i)),
            pl.BlockSpec((1, sp, kp.shape[-1]), lambda bi, i: (bi, 0, 0), pipeline_mode=pl.Buffered(1)),
            pl.BlockSpec((1, sp, LANE), lambda bi, i: (bi, 0, 0), pipeline_mode=pl.Buffered(1)),
            pl.BlockSpec((1,) + vt.shape[1:], lambda bi, i: (bi, 0, 0, 0, 0), pipeline_mode=pl.Buffered(1)),
            _const_spec(bias.shape),
            _const_spec(top.shape),
        ],
        out_specs=pl.BlockSpec((1, QA, d), lambda bi, i: (bi, i, 0)),
        out_shape=jax.ShapeDtypeStruct((b, s, d), qt.dtype),
        scratch_shapes=[
            pltpu.VMEM((sp, QA), I32),
            pltpu.VMEM((TI, IDX_HEADS * QA), F32),
            pltpu.VMEM((TI, IDX_HEADS * QA), F32),
            pltpu.VMEM((2, TA, Q_PER_KV * QA), F32),
            pltpu.VMEM((2, TA, Q_PER_KV * QA), qt.dtype),
            pltpu.VMEM((KV_GROUPS, LANE, Q_PER_KV * QA), qt.dtype),
            pltpu.VMEM((LANE, IDX_HEADS * QA), qit.dtype),
            pltpu.VMEM((nh, VR, QA), F32),
            pltpu.VMEM((nh, 1, QA), F32),
            pltpu.VMEM((nh, 1, QA), F32),
            pltpu.VMEM((TA, QA), F32),
            pltpu.VMEM((TI, QA), I32),
            pltpu.VMEM((1, QA), I32),
            pltpu.VMEM((1, QA), I32),
            pltpu.VMEM((1, QA), F32),
            pltpu.VMEM((1, QA), F32),
        ],
        compiler_params=_cparams(("arbitrary", "arbitrary")),
        name="attn_sparse",
    )(qt, qit, wit, kp, kip, vt, bias, top)


HALO = 16


def _ffn_kernel(h_ref, hh_ref, m_ref, mh_ref, wo_ref, g_ref, wup_ref, cw_ref, cb_ref, wdn_ref, o_ref,
                me_scr, n_scr, h1_scr, u_scr, acc_scr, *, tm, dff, cw, seq):
    i = pl.program_id(0)
    first = (i * tm) % seq == 0
    me_scr[0:HALO] = mh_ref[...]
    me_scr[HALO:] = m_ref[...]
    h1_scr[...] = _dot(me_scr[...], wo_ref[...])
    h1_scr[0:HALO] = h1_scr[0:HALO] + hh_ref[...]
    h1_scr[HALO:] = h1_scr[HALO:] + h_ref[...]
    h1 = h1_scr[...]
    ms = jnp.mean(h1 * h1, axis=-1, keepdims=True)
    n = h1 * lax.rsqrt(ms + EPS) * g_ref[...]
    row = lax.broadcasted_iota(I32, n.shape, 0)
    n_scr[...] = jnp.where((row < HALO) & first, 0.0, n).astype(n_scr.dtype)
    acc_scr[...] = jnp.zeros(acc_scr.shape, F32)

    def up(c):
        for part in range(2):
            c0 = part * dff + c * cw
            u_scr[c % 2, part] = _dot(n_scr[...], wup_ref[:, c0:c0 + cw])

    up(0)
    for c in range(dff // cw):
        if c + 1 < dff // cw:
            up(c + 1)
        ys = []
        for part in range(2):
            c0 = part * dff + c * cw
            u = u_scr.at[c % 2, part]
            y = cb_ref[:, c0:c0 + cw] + cw_ref[0:1, c0:c0 + cw] * u[HALO - 2:HALO - 2 + tm, :]
            y = y + cw_ref[1:2, c0:c0 + cw] * u[HALO - 1:HALO - 1 + tm, :]
            y = y + cw_ref[2:3, c0:c0 + cw] * u[HALO:HALO + tm, :]
            ys.append(y)
        a, gte = ys
        act = (gte * (1.0 / (1.0 + jnp.exp(-gte)))) * a
        acc_scr[...] += _dot(act.astype(n_scr.dtype), wdn_ref[c * cw:(c + 1) * cw, :])
    o_ref[...] = h1_scr[HALO:] + acc_scr[...]


def _out_ffn(h, m, w_out, g, w_up, conv_w, conv_b, w_down, seq, tm=512, cw=256):
    n, d = h.shape
    dff = w_down.shape[0]
    hb = tm // HALO
    halo = lambda i: (jnp.maximum(i * hb - 1, 0), 0)
    return pl.pallas_call(
        functools.partial(_ffn_kernel, tm=tm, dff=dff, cw=cw, seq=seq),
        grid=(n // tm,),
        in_specs=[
            pl.BlockSpec((tm, d), lambda i: (i, 0)),
            pl.BlockSpec((HALO, d), halo),
            pl.BlockSpec((tm, d), lambda i: (i, 0)),
            pl.BlockSpec((HALO, d), halo),
            _const_spec((d, d)),
            _const_spec((1, d)),
            _const_spec((d, 2 * dff)),
            _const_spec((3, 2 * dff)),
            _const_spec((1, 2 * dff)),
            _const_spec((dff, d)),
        ],
        out_specs=pl.BlockSpec((tm, d), lambda i: (i, 0)),
        out_shape=jax.ShapeDtypeStruct((n, d), F32),
        scratch_shapes=[
            pltpu.VMEM((tm + HALO, d), BF16),
            pltpu.VMEM((tm + HALO, d), BF16),
            pltpu.VMEM((tm + HALO, d), F32),
            pltpu.VMEM((2, 2, tm + HALO, cw), F32),
            pltpu.VMEM((tm, d), F32),
        ],
        compiler_params=_cparams(("arbitrary",)),
        name="out_ffn",
    )(h, h, m, m, w_out.astype(BF16), g.astype(F32).reshape(1, d), w_up.astype(BF16),
      conv_w.astype(F32), conv_b.astype(F32).reshape(1, -1), w_down.astype(BF16))


_HEAD_PERM = np.array([8 * gp + 4 * half + r for gp in range(2) for r in range(4) for half in range(2)])
_COL_PERM = (_HEAD_PERM[:, None] * HEAD_DIM + np.arange(HEAD_DIM)[None, :]).reshape(-1)


def _pad_front(x, b, s, pad):
    return jnp.pad(x.reshape(b, s, x.shape[-1]), ((0, 0), (pad, 0), (0, 0)))


def _mixer_a(h2, b, s, g_attn, w_in, q_g, k_g, bias, bias_max):
    ad = Q_PER_KV * KV_GROUPS * HEAD_DIM
    kd = KV_GROUPS * HEAD_DIM
    o_q, o_k, o_v, o_qi = 0, ad, ad + kd, ad + 2 * kd
    o_ki = o_qi + IDX_HEADS * IDX_DIM
    o_wi = o_ki + IDX_DIM
    w_ki = w_in[:, o_ki:o_wi]
    w_wi = jnp.pad(w_in[:, o_wi:o_wi + IDX_HEADS], ((0, 0), (0, LANE - IDX_HEADS)))
    wi_scale = IDX_HEADS ** -0.5 * IDX_DIM ** -0.5
    ones = lambda c: jnp.ones((c,), F32)
    segs = [
        (w_in[:, o_q:o_k][:, _COL_PERM], jnp.tile(q_g, ad // HEAD_DIM) * (HEAD_DIM ** -0.5 * LOG2E), True, BF16),
        (w_in[:, o_k:o_v], jnp.tile(k_g, KV_GROUPS), True, BF16),
        (w_in[:, o_v:o_qi], ones(kd), False, BF16),
        (w_in[:, o_qi:o_ki], ones(IDX_HEADS * IDX_DIM), False, BF16),
        (jnp.concatenate([w_ki, w_ki], axis=1), ones(LANE), True, BF16),
        (w_wi, ones(LANE) * wi_scale, False, F32),
    ]
    q, k, v, qi, ki, wi = _in_proj(h2, g_attn, segs)
    tr = lambda x: x.reshape(b, s, x.shape[-1]).transpose(0, 2, 1)
    sp = s + PAD_A
    vt = jnp.pad(tr(v), ((0, 0), (0, 0), (PAD_A, 0))).reshape(b, KV_GROUPS, HEAD_DIM, sp)
    vt = jnp.concatenate([vt, jnp.ones((b, KV_GROUPS, VR - HEAD_DIM, sp), vt.dtype)], axis=2)
    vt = vt.reshape(b, KV_GROUPS, VR, sp // TI, TI).transpose(0, 1, 3, 2, 4)
    top = (HEAD_DIM ** 0.5 * LOG2E * BOUND_SLACK) * jnp.max(jnp.abs(q_g)) * jnp.max(jnp.abs(k_g)) + bias_max
    o = _attn_a(tr(q), tr(qi), tr(wi[:, :IDX_HEADS]), _pad_front(k, b, s, PAD_A), _pad_front(ki, b, s, PAD_A),
                vt, bias, jnp.broadcast_to(top.astype(F32), (1, LANE)))
    return o.reshape(b * s, ad)


def _mixer_b(h2, b, s, g_attn, w_in, q_g, k_g, bias):
    d = w_in.shape[1] // 3
    nh = d // HEAD_DIM
    segs = [
        (w_in[:, :d], jnp.tile(q_g, nh) * (HEAD_DIM ** -0.5 * LOG2E), True, BF16),
        (w_in[:, d:2 * d], jnp.tile(k_g, nh), True, BF16),
        (w_in[:, 2 * d:], jnp.ones((d,), F32), False, BF16),
    ]
    q, k, v = _in_proj(h2, g_attn, segs)
    npair = d // LANE
    vp = _pad_front(v, b, s, PAD_B).reshape(b, s + PAD_B, npair, LANE)
    valid = (jnp.arange(s + PAD_B) >= PAD_B).astype(vp.dtype)
    vp = jnp.concatenate([vp, jnp.broadcast_to(valid[None, :, None, None], vp.shape)], axis=-1)
    o = _attn_b(q.reshape(b, s, d), _pad_front(k, b, s, PAD_B), vp.reshape(b, s + PAD_B, 2 * d), bias)
    return o.reshape(b * s, d)


def kernel(x, attn_norm_g, w_in_a, w_in_b, q_norm_g, k_norm_g, t5_bias, rel_bias_b, w_out, ffn_norm_g, w_up, conv_w, conv_b, w_down):
    b, s, d = x.shape
    depth = attn_norm_g.shape[0]
    assert s % SB == 0 and s % QA == 0 and min(TOPK_MAX, s // 4) == TOPK_MAX
    h = x.reshape(b * s, d)
    bias_a, bias_a_max = _bias_a(t5_bias)
    for i in range(depth):
        if i % 2 == 0:
            m = _mixer_a(h, b, s, attn_norm_g[i], w_in_a[i // 2], q_norm_g[i], k_norm_g[i], bias_a, bias_a_max)
            wo = w_out[i][_COL_PERM, :]
        else:
            m = _mixer_b(h, b, s, attn_norm_g[i], w_in_b[i // 2], q_norm_g[i], k_norm_g[i],
                         _bias_b(rel_bias_b[i // 2], q_norm_g[i], k_norm_g[i]))
            wo = w_out[i]
        h = _out_ffn(h, m, wo, ffn_norm_g[i], w_up[i], conv_w[i], conv_b[i], w_down[i], s)
    return h.reshape(b, s, d)
```

```python
import functools

import numpy as np
import jax
import jax.numpy as jnp
from jax import lax
from jax.experimental import pallas as pl
from jax.experimental.pallas import tpu as pltpu

F32 = jnp.float32
BF16 = jnp.bfloat16
I32 = jnp.int32

EPS = 1e-6
CHUNK = 64
HEAD_DIM = 64
KV_GROUPS = 4
Q_PER_KV = 4
IDX_HEADS = 8
IDX_DIM = 64
TOPK_MAX = 256
T5_BUCKETS = 32
LEFT_CHUNKS = 8
REL_CLIP = 256

LANE = 128
SUBLANE = 8
V7X_VMEM_BYTES = 64 * 2**20
VMEM_LIMIT = 56 * 2**20

NEG = -1e30
LOG2E = 1.4426950408889634
INT_MIN = -2**31
INT_MAX = 2**31 - 1

QA = 256
PAD_A = 256
TI = 256
TA = 512
VR = HEAD_DIM + 16
CB = 64
KB = 16
BISECT_UNROLL = 2
SNAP_SPAN = 8.0
BOUND_SLACK = 1.02
SUM_FLOOR = 2.0 ** -60
QB = 128
SB = 1024
PAD_B = LEFT_CHUNKS * CHUNK
WIN_B = PAD_B + QB
UNROLL_B = 8


def _cparams(sem):
    return pltpu.CompilerParams(dimension_semantics=sem, vmem_limit_bytes=VMEM_LIMIT)


def _const_spec(shape):
    nd = len(shape)
    return pl.BlockSpec(shape, lambda *_: (0,) * nd, pipeline_mode=pl.Buffered(1))


def _lo_half(shape):
    return (lax.broadcasted_iota(I32, shape, len(shape) - 1) & HEAD_DIM) == 0


def _dot_t(a, b):
    return lax.dot_general(a, b, (((1,), (1,)), ((), ())), preferred_element_type=F32)


def _dot(a, b):
    return jnp.dot(a, b, preferred_element_type=F32)


def _head_rms(y):
    lo = _lo_half(y.shape)
    z = y * y
    sa = jnp.sum(jnp.where(lo, z, 0.0), axis=-1, keepdims=True)
    sb = jnp.sum(jnp.where(lo, 0.0, z), axis=-1, keepdims=True)
    inv = jnp.where(lo, lax.rsqrt(sa * (1.0 / HEAD_DIM) + EPS), lax.rsqrt(sb * (1.0 / HEAD_DIM) + EPS))
    return y * inv


def _proj_kernel(x_ref, g_ref, *rest, segs, cw):
    ns = len(segs)
    w_refs, s_refs, o_refs = rest[:ns], rest[ns:2 * ns], rest[2 * ns:3 * ns]
    n_scr = rest[3 * ns]
    x = x_ref[...]
    ms = jnp.mean(x * x, axis=-1, keepdims=True)
    n_scr[...] = (x * lax.rsqrt(ms + EPS) * g_ref[...]).astype(n_scr.dtype)
    for (cols, headnorm, transposed), w_ref, s_ref, o_ref in zip(segs, w_refs, s_refs, o_refs):
        for c0 in range(0, cols, cw):
            c1 = min(c0 + cw, cols)
            if transposed:
                yt = _dot_t(w_ref[c0:c1, :], n_scr[...])
                if headnorm:
                    heads = []
                    for r0 in range(0, c1 - c0, HEAD_DIM):
                        z = yt[r0:r0 + HEAD_DIM]
                        ss = jnp.sum(z * z, axis=0, keepdims=True)
                        heads.append(z * lax.rsqrt(ss * (1.0 / HEAD_DIM) + EPS))
                    yt = jnp.concatenate(heads, axis=0)
                o_ref[0, c0:c1, :] = (yt * s_ref[c0:c1, :]).astype(o_ref.dtype)
                continue
            y = _dot(n_scr[...], w_ref[:, c0:c1])
            for l0 in range(0, c1 - c0, LANE):
                yl = y[:, l0:l0 + LANE]
                if headnorm:
                    yl = _head_rms(yl)
                o_ref[:, c0 + l0:c0 + l0 + LANE] = (yl * s_ref[:, c0 + l0:c0 + l0 + LANE]).astype(o_ref.dtype)


def _in_proj(x, g, segs, batch, tm=512, cw=256):
    n, d = x.shape
    tpb = n // batch // tm
    ws = [(s[0].T if s[4] else s[0]).astype(BF16) for s in segs]
    ss = [s[1].astype(F32).reshape((-1, 1) if s[4] else (1, -1)) for s in segs]
    meta = tuple((int(s[0].shape[1]), bool(s[2]), bool(s[4])) for s in segs)
    in_specs = [pl.BlockSpec((tm, d), lambda i: (i, 0)), _const_spec((1, d))]
    in_specs += [_const_spec(w.shape) for w in ws] + [_const_spec(s.shape) for s in ss]
    out_specs = [pl.BlockSpec((1, c, tm), lambda i: (i // tpb, 0, i % tpb)) if t else pl.BlockSpec((tm, c), lambda i: (i, 0))
                 for c, _, t in meta]
    out_shape = [jax.ShapeDtypeStruct((batch, c, n // batch) if t else (n, c), s[3]) for (c, _, t), s in zip(meta, segs)]
    return pl.pallas_call(
        functools.partial(_proj_kernel, segs=meta, cw=cw),
        grid=(n // tm,),
        in_specs=in_specs,
        out_specs=out_specs,
        out_shape=out_shape,
        scratch_shapes=[pltpu.VMEM((tm, d), BF16)],
        compiler_params=_cparams(("arbitrary",)),
        name="in_proj",
    )(x, g.astype(F32).reshape(1, d), *ws, *ss)


def _toeplitz_kernel(vec_ref, off_ref, o_ref, *, rows, cols, band):
    n = vec_ref.shape[-1]
    x = jnp.broadcast_to(vec_ref[0], (rows, n))
    t = (pltpu.roll(x, 0, 1, stride=1, stride_axis=0)[:, :cols] - off_ref[0][:, :1]) * LOG2E
    if band is not None:
        r = lax.broadcasted_iota(I32, (rows, cols), 0)
        c = lax.broadcasted_iota(I32, (rows, cols), 1)
        lo = (r // CHUNK) * CHUNK
        t = jnp.where((c >= lo) & (c < lo + band), t, NEG)
    o_ref[0] = t


def _toeplitz(vec, off, rows, cols, band=None):
    h, n = vec.shape
    return pl.pallas_call(
        functools.partial(_toeplitz_kernel, rows=rows, cols=cols, band=band),
        grid=(h,),
        in_specs=[pl.BlockSpec((1, 1, n), lambda i: (i, 0, 0)), pl.BlockSpec((1, 1, LANE), lambda i: (i, 0, 0))],
        out_specs=pl.BlockSpec((1, rows, cols), lambda i: (i, 0, 0)),
        out_shape=jax.ShapeDtypeStruct((h, rows, cols), F32),
        compiler_params=_cparams(("arbitrary",)),
        name="toeplitz_bias",
    )(vec.reshape(h, 1, n), jnp.broadcast_to(off.reshape(h, 1, 1), (h, 1, LANE)))


def _t5_bucket_static(rel):
    n = np.abs(rel)
    large = 8 + sum((n >= t).astype(np.int64) for t in (12, 16, 23, 32, 46, 64, 91))
    return np.where(rel > 0, 16, 0) + np.where(n < 8, n, large)


def _bias_a(t5_table):
    n = 2 * TA
    j = np.arange(n)
    d = np.where(j < n // 2, j, j - n)
    bucket = _t5_bucket_static(-d - PAD_A)
    vec = t5_table[bucket, :].T.astype(F32)
    far = t5_table[T5_BUCKETS // 2 - 1, :].astype(F32)
    bias_max = jnp.maximum(jnp.max(t5_table.astype(F32) - far[None, :]), 0.0) * LOG2E
    return _toeplitz(vec, far, TA, QA), bias_max


def _bias_b(rel_table, q_g, k_g):
    n = 1024
    j = np.arange(n)
    d = np.where(j < WIN_B, j, j - n)
    idx = np.clip(PAD_B - d, -REL_CLIP, REL_CLIP) + REL_CLIP
    vec = rel_table[idx, :].T.astype(F32)
    top = (HEAD_DIM ** 0.5 * BOUND_SLACK) * jnp.max(jnp.abs(q_g)) * jnp.max(jnp.abs(k_g)) + jnp.maximum(jnp.max(rel_table), 0.0)
    return _toeplitz(vec, jnp.broadcast_to(top.astype(F32), (vec.shape[0],)), QB, WIN_B, band=PAD_B + CHUNK)


def _attn_b_kernel(q_ref, k_ref, v_ref, ok_ref, b_ref, o_ref):
    j = pl.program_id(2)

    def values(q0):
        return jnp.concatenate([v_ref[0, pl.ds(q0, WIN_B), :], ok_ref[pl.ds(q0, WIN_B), :]], axis=1)

    lo = _lo_half((QB, LANE))
    bias = b_ref[...].reshape(2 * QB, WIN_B)
    col = lax.broadcasted_iota(I32, (2 * QB, WIN_B), 1)

    def logits(t):
        q0 = pl.multiple_of(j * SB + t * QB, QB)
        qs = q_ref[0, pl.ds(pl.multiple_of(t * QB, QB), QB), :]
        zero = jnp.zeros_like(qs)
        qq = jnp.concatenate([jnp.where(lo, qs, zero), jnp.where(lo, zero, qs)], axis=0)
        return q0, _dot_t(qq, k_ref[0, pl.ds(q0, WIN_B), :]) + bias

    def store(t, pv):
        o = pv[:, :LANE] / pv[:, LANE:]
        o_ref[0, pl.ds(pl.multiple_of(t * QB, QB), QB), :] = jnp.where(lo, o[:QB], o[QB:]).astype(o_ref.dtype)

    def body(t2, carry):
        sums = None
        for k in range(UNROLL_B):
            t = t2 * UNROLL_B + k
            q0, s = logits(t)
            pv = _dot(jnp.exp2(s).astype(v_ref.dtype), values(q0))
            store(t, pv)
            sums = pv[:, LANE:] if sums is None else jnp.minimum(sums, pv[:, LANE:])

        @pl.when(jnp.min(sums) < SUM_FLOOR)
        def _():
            for k in range(UNROLL_B):
                t = t2 * UNROLL_B + k
                q0, s = logits(t)
                s = jnp.where(col >= PAD_B - q0, s, NEG)
                p = jnp.exp2(s - jnp.max(s, axis=-1, keepdims=True))
                store(t, _dot(p.astype(v_ref.dtype), values(q0)))

        return carry

    lax.fori_loop(0, SB // QB // UNROLL_B, body, 0)


def _attn_b(q, kp, vp, bias):
    b, s, d = q.shape
    npair = d // LANE
    ok = jnp.broadcast_to((jnp.arange(s + PAD_B) >= PAD_B).astype(vp.dtype)[:, None], (s + PAD_B, LANE))
    return pl.pallas_call(
        _attn_b_kernel,
        grid=(b, npair, s // SB),
        in_specs=[
            pl.BlockSpec((1, SB, LANE), lambda bi, p, j: (bi, j, p)),
            pl.BlockSpec((1, s + PAD_B, LANE), lambda bi, p, j: (bi, 0, p)),
            pl.BlockSpec((1, s + PAD_B, LANE), lambda bi, p, j: (bi, 0, p)),
            _const_spec((s + PAD_B, LANE)),
            pl.BlockSpec((2, QB, WIN_B), lambda bi, p, j: (p, 0, 0)),
        ],
        out_specs=pl.BlockSpec((1, SB, LANE), lambda bi, p, j: (bi, j, p)),
        out_shape=jax.ShapeDtypeStruct((b, s, d), q.dtype),
        compiler_params=_cparams(("arbitrary", "arbitrary", "arbitrary")),
        name="attn_band",
    )(q, kp, vp, ok, bias)


KEY_FINITE_MAX = 0x7F7FFFFF
KEY_FINITE_MIN = -0x7F800000


def _score_to_key(x):
    bits = lax.bitcast_convert_type(x, I32)
    return bits ^ ((bits >> 31) & 0x7FFFFFFF)


def _key_to_score(k):
    k = jnp.minimum(jnp.maximum(k, KEY_FINITE_MIN), KEY_FINITE_MAX)
    return lax.bitcast_convert_type(k ^ ((k >> 31) & 0x7FFFFFFF), F32)


def _count(keys, start, ntile, pred):
    nacc = 4

    def tile(t, accs):
        accs = list(accs)
        base = start + t * TA
        for cb in range(TA // CB):
            blk = keys[pl.ds(base + cb * CB, CB), :]
            for c in range(CB // SUBLANE):
                kk = blk[c * SUBLANE:(c + 1) * SUBLANE]
                a = accs[c % nacc]
                accs[c % nacc] = jnp.where(pred(kk, base + cb * CB + c * SUBLANE), a + 1, a)
        return tuple(accs)

    accs = lax.fori_loop(0, ntile, tile, tuple(jnp.zeros((SUBLANE, QA), I32) for _ in range(nacc)))
    tot = (accs[0] + accs[1]) + (accs[2] + accs[3])
    return jnp.sum(tot, axis=0, keepdims=True).astype(F32)


def _snap(keys, start, ntile, lo, hi):
    def tile(t, accs):
        mn, mx = list(accs[:2]), list(accs[2:])
        base = start + t * TA
        for cb in range(TA // CB):
            blk = keys[pl.ds(base + cb * CB, CB), :]
            for c in range(CB // SUBLANE):
                kk = blk[c * SUBLANE:(c + 1) * SUBLANE]
                mn[c % 2] = jnp.minimum(mn[c % 2], jnp.where(kk >= lo, kk, INT_MAX))
                mx[c % 2] = jnp.maximum(mx[c % 2], jnp.where(kk < hi, kk, INT_MIN))
        return tuple(mn + mx)

    top = jnp.full((SUBLANE, QA), INT_MAX, I32)
    bot = jnp.full((SUBLANE, QA), INT_MIN, I32)
    mn0, mn1, mx0, mx1 = lax.fori_loop(0, ntile, tile, (top, top, bot, bot))
    return (jnp.min(jnp.minimum(mn0, mn1), axis=0, keepdims=True),
            jnp.max(jnp.maximum(mx0, mx1), axis=0, keepdims=True))


def _demote_ties(keys, start, ntile, thr, keep, tie):
    rid = lax.broadcasted_iota(I32, (SUBLANE, QA), 0)

    def block(c, seen):
        r0 = start + c * CB
        blk = keys[pl.ds(r0, CB), :]
        out = []
        for s in range(CB // SUBLANE):
            kk = blk[s * SUBLANE:(s + 1) * SUBLANE]
            eq = (kk == thr) & tie
            e = jnp.where(eq, 1, 0)
            for sh in (1, 2, 4):
                e = e + jnp.where(rid >= sh, pltpu.roll(e, sh, 0), 0)
            out.append(jnp.where(eq & (seen + e > keep), INT_MIN, kk))
            seen = seen + e[SUBLANE - 1:SUBLANE]
        keys[pl.ds(r0, CB), :] = jnp.concatenate(out, axis=0)
        return seen

    lax.fori_loop(0, ntile * (TA // CB), block, jnp.zeros((1, QA), I32))


def _attn_a_kernel(qt_ref, qit_ref, wit_ref, k_ref, ki_ref, vt_ref, b_ref, top_ref, o_ref,
                   keys, x0_scr, x1_scr, s_scr, p_scr, qs_scr, qis_scr, acc_scr, m_scr, al_scr, mb_scr, gm_scr,
                   lo_scr, hi_scr, cl_scr, ch_scr):
    i = pl.program_id(1)
    topk = float(TOPK_MAX)
    sub_lo = lax.broadcasted_iota(I32, (LANE, QA), 0) < HEAD_DIM

    for t in range(8):
        qt = qt_ref[0, t * LANE:(t + 1) * LANE, :]
        gp, r = divmod(t, 4)
        zero = jnp.zeros_like(qt)
        qs_scr[2 * gp, :, r * QA:(r + 1) * QA] = jnp.where(sub_lo, qt, zero)
        qs_scr[2 * gp + 1, :, r * QA:(r + 1) * QA] = jnp.where(sub_lo, zero, qt)
    for p in range(IDX_HEADS // 2):
        qt = qit_ref[0, p * LANE:(p + 1) * LANE, :]
        zero = jnp.zeros_like(qt)
        qis_scr[:, (2 * p) * QA:(2 * p + 1) * QA] = jnp.where(sub_lo, qt, zero)
        qis_scr[:, (2 * p + 1) * QA:(2 * p + 2) * QA] = jnp.where(sub_lo, zero, qt)

    gm_scr[...] = jnp.full((TI, QA), INT_MIN, I32)
    qcol = lax.broadcasted_iota(I32, (CB, QA), 1)
    krow = lax.broadcasted_iota(I32, (CB, QA), 0)
    last = i + 1
    x_scr = (x0_scr, x1_scr)

    def score_dot(tt, buf):
        x_scr[buf][...] = _dot(ki_ref[0, pl.ds(pl.multiple_of(tt * TI, TI), TI), :], qis_scr[...])

    def score_keys(tt, buf):
        base = pl.multiple_of(tt * TI, TI)
        for cb in range(TI // CB):
            sc = jnp.zeros((CB, QA), F32)
            for h in range(IDX_HEADS):
                sc = sc + wit_ref[0, h:h + 1, :] * jnp.maximum(x_scr[buf][cb * CB:(cb + 1) * CB, h * QA:(h + 1) * QA], 0.0)
            causal = (krow + cb * CB < (qcol // CHUNK + 1) * CHUNK) | (tt < last)
            key = jnp.where(causal & (tt > 0), _score_to_key(sc), INT_MIN)
            keys[pl.ds(base + cb * CB, CB), :] = key
            gm_scr[cb * CB:(cb + 1) * CB, :] = jnp.maximum(gm_scr[cb * CB:(cb + 1) * CB, :], key)

    first = i & 1
    score_dot(first, 0)

    def score_pair(p, carry):
        ta = first + 2 * p
        score_dot(ta + 1, 1)
        score_keys(ta, 0)
        score_dot(jnp.minimum(ta + 2, last), 0)
        score_keys(ta + 1, 1)
        return carry

    lax.fori_loop(0, (i + 2 - first) // 2, score_pair, 0)

    gm = gm_scr[...]
    s_start = pl.multiple_of((i & 1) * PAD_A, PAD_A)
    s_ntile = (i + 2) // 2
    lo0 = jnp.min(gm, axis=0, keepdims=True)
    lo_scr[...] = lo0
    hi_scr[...] = jnp.max(gm, axis=0, keepdims=True) + 1
    cl_scr[...] = _count(keys, s_start, s_ntile, lambda kk, _: kk >= lo0)
    ch_scr[...] = jnp.zeros((1, QA), F32)

    def open_span():
        opn = hi_scr[...] - 1 > lo_scr[...]
        return jnp.max(jnp.where(opn, jnp.maximum(cl_scr[...] - ch_scr[...], 1.0), 0.0))

    def search(state):
        span, prev = state

        @pl.when((span <= SNAP_SPAN) | (span >= prev))
        def _():
            lo, hi = lo_scr[...], hi_scr[...]
            amin, amax = _snap(keys, s_start, s_ntile, lo, hi)
            opn = hi - 1 > lo
            lo_scr[...] = jnp.where(opn, amin, lo)
            hi_scr[...] = jnp.where(opn, amax + 1, hi)

        for by_value in (True, False):
            lo, hi = lo_scr[...], hi_scr[...]
            cl, ch = cl_scr[...], ch_scr[...]
            if by_value:
                fm = 0.5 * _key_to_score(lo) + 0.5 * _key_to_score(hi)
                mid = jnp.where(hi - 1 > lo, jnp.minimum(jnp.maximum(_score_to_key(fm), lo + 1), hi - 1), lo)
            else:
                mid = (lo >> 1) + (hi >> 1) + (lo & hi & 1)
            cnt = _count(keys, s_start, s_ntile, lambda kk, _: kk >= mid)
            ge = cnt >= topk
            exact = cnt == topk
            lo_scr[...] = jnp.where(ge, mid, lo)
            cl_scr[...] = jnp.where(ge, cnt, cl)
            hi_scr[...] = jnp.where(exact, mid + 1, jnp.where(ge, hi, mid))
            ch_scr[...] = jnp.where(ge, ch, cnt)
        return open_span(), span

    lax.while_loop(lambda st: st[0] > 0.0, search, (open_span(), jnp.float32(3e38)))

    thr = lo_scr[...]
    tie = (cl_scr[...] > topk) & (thr > INT_MIN)

    @pl.when(jnp.max(jnp.where(tie, 1.0, 0.0)) > 0.0)
    def _():
        _demote_ties(keys, s_start, s_ntile, thr, (topk - ch_scr[...]).astype(I32), tie)

    thr = jnp.maximum(thr, INT_MIN + 1)

    top = top_ref[0:1, 0:1]

    def attend(start, with_bias, running_max):
        shift = 0.0 if running_max else -top
        for cb in range(TA // CB):
            rows = pl.ds(start + cb * CB, CB)
            mb_scr[cb * CB:(cb + 1) * CB, :] = jnp.where(keys[rows, :] >= thr, shift, NEG)
        blk = start // TI

        def qk(g):
            s_scr[g % 2] = _dot(k_ref[0, pl.ds(start, TA), (g // 2) * LANE:(g // 2 + 1) * LANE], qs_scr[g])

        qk(0)
        for g in range(KV_GROUPS):
            buf = g % 2
            if g + 1 < KV_GROUPS:
                qk(g + 1)
            hds = [g * Q_PER_KV + r for r in range(Q_PER_KV)]

            def logits(kb):
                rows = slice(kb * KB, (kb + 1) * KB)
                mb = mb_scr[rows, :]
                out = []
                for r in range(Q_PER_KV):
                    s = s_scr[buf, rows, r * QA:(r + 1) * QA] + mb
                    if with_bias:
                        s = s + b_ref[hds[r], rows, :]
                    out.append(s)
                return out

            if running_max:
                mx = logits(0)
                for kb in range(1, TA // KB):
                    mx = [jnp.maximum(a, s) for a, s in zip(mx, logits(kb))]
                m_new = []
                for r in range(Q_PER_KV):
                    m_old = m_scr[hds[r]]
                    mn = jnp.maximum(m_old, jnp.max(mx[r], axis=0, keepdims=True))
                    al_scr[hds[r]] = jnp.exp2(m_old - mn)
                    m_scr[hds[r]] = mn
                    m_new.append(jnp.broadcast_to(mn, (KB, QA)))
            for kb in range(TA // KB):
                ss = logits(kb)
                for r in range(Q_PER_KV):
                    t = ss[r] - m_new[r] if running_max else ss[r]
                    p_scr[buf, kb * KB:(kb + 1) * KB, r * QA:(r + 1) * QA] = jnp.exp2(t).astype(p_scr.dtype)
            for r in range(Q_PER_KV):
                cols = slice(r * QA, (r + 1) * QA)
                pv = _dot(vt_ref[0, g, blk], p_scr[buf, 0:TI, cols]) + _dot(vt_ref[0, g, blk + 1], p_scr[buf, TI:TA, cols])
                if running_max:
                    acc_scr[hds[r]] = al_scr[hds[r]] * acc_scr[hds[r]] + pv
                else:
                    acc_scr[hds[r]] = acc_scr[hds[r]] + pv

    def sweep(running_max):
        acc_scr[...] = jnp.zeros(acc_scr.shape, F32)

        def far_body(t, carry):
            attend(pl.multiple_of((i & 1) * PAD_A + t * TA, PAD_A), False, running_max)
            return carry

        lax.fori_loop(0, i // 2, far_body, 0)
        attend(pl.multiple_of(i * QA, QA), True, running_max)

    sweep(False)
    lmin = acc_scr[0, HEAD_DIM:HEAD_DIM + 1, :]
    for hd in range(1, acc_scr.shape[0]):
        lmin = jnp.minimum(lmin, acc_scr[hd, HEAD_DIM:HEAD_DIM + 1, :])

    @pl.when(jnp.min(lmin) < SUM_FLOOR)
    def _():
        m_scr[...] = jnp.full(m_scr.shape, NEG, F32)
        sweep(True)

    for t in range(8):
        gp, r = divmod(t, 4)
        aa = acc_scr[(2 * gp) * Q_PER_KV + r]
        ab = acc_scr[(2 * gp + 1) * Q_PER_KV + r]
        oa = aa[:HEAD_DIM] / aa[HEAD_DIM:HEAD_DIM + 1]
        ob = ab[:HEAD_DIM] / ab[HEAD_DIM:HEAD_DIM + 1]
        o_ref[0, :, t * LANE:(t + 1) * LANE] = jnp.concatenate([oa, ob], axis=0).T.astype(o_ref.dtype)


def _attn_a(qt, qit, wit, kp, kip, vt, bias, top):
    b, d, s = qt.shape
    sp = s + PAD_A
    nh = d // HEAD_DIM
    return pl.pallas_call(
        _attn_a_kernel,
        grid=(b, s // QA),
        in_specs=[
            pl.BlockSpec((1, d, QA), lambda bi, i: (bi, 0, i)),
            pl.BlockSpec((1, qit.shape[1], QA), lambda bi, i: (bi, 0, i)),
            pl.BlockSpec((1, IDX_HEADS, QA), lambda bi, i: (bi, 0, i)),
            pl.BlockSpec((1, sp, kp.shape[-1]), lambda bi, i: (bi, 0, 0), pipeline_mode=pl.Buffered(1)),
            pl.BlockSpec((1, sp, LANE), lambda bi, i: (bi, 0, 0), pipeline_mode=pl.Buffered(1)),
            pl.BlockSpec((1,) + vt.shape[1:], lambda bi, i: (bi, 0, 0, 0, 0), pipeline_mode=pl.Buffered(1)),
            _const_spec(bias.shape),
            _const_spec(top.shape),
        ],
        out_specs=pl.BlockSpec((1, QA, d), lambda bi, i: (bi, i, 0)),
        out_shape=jax.ShapeDtypeStruct((b, s, d), qt.dtype),
        scratch_shapes=[
            pltpu.VMEM((sp, QA), I32),
            pltpu.VMEM((TI, IDX_HEADS * QA), F32),
            pltpu.VMEM((TI, IDX_HEADS * QA), F32),
            pltpu.VMEM((2, TA, Q_PER_KV * QA), F32),
            pltpu.VMEM((2, TA, Q_PER_KV * QA), qt.dtype),
            pltpu.VMEM((KV_GROUPS, LANE, Q_PER_KV * QA), qt.dtype),
            pltpu.VMEM((LANE, IDX_HEADS * QA), qit.dtype),
            pltpu.VMEM((nh, VR, QA), F32),
            pltpu.VMEM((nh, 1, QA), F32),
            pltpu.VMEM((nh, 1, QA), F32),
            pltpu.VMEM((TA, QA), F32),
            pltpu.VMEM((TI, QA), I32),
            pltpu.VMEM((1, QA), I32),
            pltpu.VMEM((1, QA), I32),
            pltpu.VMEM((1, QA), F32),
            pltpu.VMEM((1, QA), F32),
        ],
        compiler_params=_cparams(("arbitrary", "arbitrary")),
        name="attn_sparse",
    )(qt, qit, wit, kp, kip, vt, bias, top)


HALO = 16


def _ffn_kernel(h_ref, hh_ref, m_ref, mh_ref, wo_ref, g_ref, wup_ref, cw_ref, cb_ref, wdn_ref, o_ref,
                me_scr, n_scr, h1_scr, u_scr, acc_scr, *, tm, dff, cw, seq):
    i = pl.program_id(0)
    first = (i * tm) % seq == 0
    me_scr[0:HALO] = mh_ref[...]
    me_scr[HALO:] = m_ref[...]
    h1_scr[...] = _dot(me_scr[...], wo_ref[...])
    h1_scr[0:HALO] = h1_scr[0:HALO] + hh_ref[...]
    h1_scr[HALO:] = h1_scr[HALO:] + h_ref[...]
    h1 = h1_scr[...]
    ms = jnp.mean(h1 * h1, axis=-1, keepdims=True)
    n = h1 * lax.rsqrt(ms + EPS) * g_ref[...]
    row = lax.broadcasted_iota(I32, n.shape, 0)
    n_scr[...] = jnp.where((row < HALO) & first, 0.0, n).astype(n_scr.dtype)
    acc_scr[...] = jnp.zeros(acc_scr.shape, F32)

    def up(c):
        for part in range(2):
            c0 = part * dff + c * cw
            u_scr[c % 2, part] = _dot(n_scr[...], wup_ref[:, c0:c0 + cw])

    up(0)
    for c in range(dff // cw):
        if c + 1 < dff // cw:
            up(c + 1)
        ys = []
        for part in range(2):
            c0 = part * dff + c * cw
            u = u_scr.at[c % 2, part]
            y = cb_ref[:, c0:c0 + cw] + cw_ref[0:1, c0:c0 + cw] * u[HALO - 2:HALO - 2 + tm, :]
            y = y + cw_ref[1:2, c0:c0 + cw] * u[HALO - 1:HALO - 1 + tm, :]
            y = y + cw_ref[2:3, c0:c0 + cw] * u[HALO:HALO + tm, :]
            ys.append(y)
        a, gte = ys
        act = (gte * (1.0 / (1.0 + jnp.exp(-gte)))) * a
        acc_scr[...] += _dot(act.astype(n_scr.dtype), wdn_ref[c * cw:(c + 1) * cw, :])
    o_ref[...] = h1_scr[HALO:] + acc_scr[...]


def _out_ffn(h, m, w_out, g, w_up, conv_w, conv_b, w_down, seq, tm=512, cw=256):
    n, d = h.shape
    dff = w_down.shape[0]
    hb = tm // HALO
    halo = lambda i: (jnp.maximum(i * hb - 1, 0), 0)
    return pl.pallas_call(
        functools.partial(_ffn_kernel, tm=tm, dff=dff, cw=cw, seq=seq),
        grid=(n // tm,),
        in_specs=[
            pl.BlockSpec((tm, d), lambda i: (i, 0)),
            pl.BlockSpec((HALO, d), halo),
            pl.BlockSpec((tm, d), lambda i: (i, 0)),
            pl.BlockSpec((HALO, d), halo),
            _const_spec((d, d)),
            _const_spec((1, d)),
            _const_spec((d, 2 * dff)),
            _const_spec((3, 2 * dff)),
            _const_spec((1, 2 * dff)),
            _const_spec((dff, d)),
        ],
        out_specs=pl.BlockSpec((tm, d), lambda i: (i, 0)),
        out_shape=jax.ShapeDtypeStruct((n, d), F32),
        scratch_shapes=[
            pltpu.VMEM((tm + HALO, d), BF16),
            pltpu.VMEM((tm + HALO, d), BF16),
            pltpu.VMEM((tm + HALO, d), F32),
            pltpu.VMEM((2, 2, tm + HALO, cw), F32),
            pltpu.VMEM((tm, d), F32),
        ],
        compiler_params=_cparams(("arbitrary",)),
        name="out_ffn",
    )(h, h, m, m, w_out.astype(BF16), g.astype(F32).reshape(1, d), w_up.astype(BF16),
      conv_w.astype(F32), conv_b.astype(F32).reshape(1, -1), w_down.astype(BF16))


_HEAD_PERM = np.array([8 * gp + 4 * half + r for gp in range(2) for r in range(4) for half in range(2)])
_COL_PERM = (_HEAD_PERM[:, None] * HEAD_DIM + np.arange(HEAD_DIM)[None, :]).reshape(-1)


def _pad_front(x, b, s, pad):
    return jnp.pad(x.reshape(b, s, x.shape[-1]), ((0, 0), (pad, 0), (0, 0)))


def _mixer_a(h2, b, s, g_attn, w_in, q_g, k_g, bias, bias_max):
    ad = Q_PER_KV * KV_GROUPS * HEAD_DIM
    kd = KV_GROUPS * HEAD_DIM
    o_q, o_k, o_v, o_qi = 0, ad, ad + kd, ad + 2 * kd
    o_ki = o_qi + IDX_HEADS * IDX_DIM
    o_wi = o_ki + IDX_DIM
    w_ki = w_in[:, o_ki:o_wi]
    wi_scale = IDX_HEADS ** -0.5 * IDX_DIM ** -0.5
    ones = lambda c: jnp.ones((c,), F32)
    segs = [
        (w_in[:, o_q:o_k][:, _COL_PERM], jnp.tile(q_g, ad // HEAD_DIM) * (HEAD_DIM ** -0.5 * LOG2E), True, BF16, True),
        (w_in[:, o_k:o_v], jnp.tile(k_g, KV_GROUPS), True, BF16, False),
        (w_in[:, o_v:o_qi], ones(kd), False, BF16, True),
        (w_in[:, o_qi:o_ki], ones(IDX_HEADS * IDX_DIM), False, BF16, True),
        (jnp.concatenate([w_ki, w_ki], axis=1), ones(LANE), True, BF16, False),
        (w_in[:, o_wi:o_wi + IDX_HEADS], ones(IDX_HEADS) * wi_scale, False, F32, True),
    ]
    qt, k, vt, qit, ki, wit = _in_proj(h2, g_attn, segs, b)
    sp = s + PAD_A
    vt = jnp.pad(vt, ((0, 0), (0, 0), (PAD_A, 0))).reshape(b, KV_GROUPS, HEAD_DIM, sp)
    vt = jnp.concatenate([vt, jnp.ones((b, KV_GROUPS, VR - HEAD_DIM, sp), vt.dtype)], axis=2)
    vt = vt.reshape(b, KV_GROUPS, VR, sp // TI, TI).transpose(0, 1, 3, 2, 4)
    top = (HEAD_DIM ** 0.5 * LOG2E * BOUND_SLACK) * jnp.max(jnp.abs(q_g)) * jnp.max(jnp.abs(k_g)) + bias_max
    o = _attn_a(qt, qit, wit, _pad_front(k, b, s, PAD_A), _pad_front(ki, b, s, PAD_A),
                vt, bias, jnp.broadcast_to(top.astype(F32), (1, LANE)))
    return o.reshape(b * s, ad)


def _mixer_b(h2, b, s, g_attn, w_in, q_g, k_g, bias):
    d = w_in.shape[1] // 3
    nh = d // HEAD_DIM
    segs = [
        (w_in[:, :d], jnp.tile(q_g, nh) * (HEAD_DIM ** -0.5 * LOG2E), True, BF16, False),
        (w_in[:, d:2 * d], jnp.tile(k_g, nh), True, BF16, False),
        (w_in[:, 2 * d:], jnp.ones((d,), F32), False, BF16, False),
    ]
    q, k, v = _in_proj(h2, g_attn, segs, b)
    o = _attn_b(q.reshape(b, s, d), _pad_front(k, b, s, PAD_B), _pad_front(v, b, s, PAD_B), bias)
    return o.reshape(b * s, d)


def kernel(x, attn_norm_g, w_in_a, w_in_b, q_norm_g, k_norm_g, t5_bias, rel_bias_b, w_out, ffn_norm_g, w_up, conv_w, conv_b, w_down):
    b, s, d = x.shape
    depth = attn_norm_g.shape[0]
    assert s % SB == 0 and s % QA == 0 and min(TOPK_MAX, s // 4) == TOPK_MAX
    h = x.reshape(b * s, d)
    bias_a, bias_a_max = _bias_a(t5_bias)
    for i in range(depth):
        if i % 2 == 0:
            m = _mixer_a(h, b, s, attn_norm_g[i], w_in_a[i // 2], q_norm_g[i], k_norm_g[i], bias_a, bias_a_max)
            wo = w_out[i][_COL_PERM, :]
        else:
            m = _mixer_b(h, b, s, attn_norm_g[i], w_in_b[i // 2], q_norm_g[i], k_norm_g[i],
                         _bias_b(rel_bias_b[i // 2], q_norm_g[i], k_norm_g[i]))
            wo = w_out[i]
        h = _out_ffn(h, m, wo, ffn_norm_g[i], w_up[i], conv_w[i], conv_b[i], w_down[i], s)
    return h.reshape(b, s, d)
```

```python
import functools

import numpy as np
import jax
import jax.numpy as jnp
from jax import lax
from jax.experimental import pallas as pl
from jax.experimental.pallas import tpu as pltpu

F32 = jnp.float32
BF16 = jnp.bfloat16
I32 = jnp.int32

EPS = 1e-6
CHUNK = 64
HEAD_DIM = 64
KV_GROUPS = 4
Q_PER_KV = 4
IDX_HEADS = 8
IDX_DIM = 64
TOPK_MAX = 256
T5_BUCKETS = 32
LEFT_CHUNKS = 8
REL_CLIP = 256

LANE = 128
SUBLANE = 8
V7X_VMEM_BYTES = 64 * 2**20
VMEM_LIMIT = 56 * 2**20

NEG = -1e30
LOG2E = 1.4426950408889634
INT_MIN = -2**31
INT_MAX = 2**31 - 1

QA = 256
PAD_A = 256
TI = 256
TA = 512
VR = HEAD_DIM + 16
CB = 64
KB = 16
BISECT_UNROLL = 2
SNAP_SPAN = 8.0
BOUND_SLACK = 1.02
SUM_FLOOR = 2.0 ** -60
QB = 128
SB = 1024
PAD_B = LEFT_CHUNKS * CHUNK
WIN_B = PAD_B + QB
UNROLL_B = 8


def _cparams(sem):
    return pltpu.CompilerParams(dimension_semantics=sem, vmem_limit_bytes=VMEM_LIMIT)


def _const_spec(shape):
    nd = len(shape)
    return pl.BlockSpec(shape, lambda *_: (0,) * nd, pipeline_mode=pl.Buffered(1))


def _lo_half(shape):
    return (lax.broadcasted_iota(I32, shape, len(shape) - 1) & HEAD_DIM) == 0


def _dot_t(a, b):
    return lax.dot_general(a, b, (((1,), (1,)), ((), ())), preferred_element_type=F32)


def _dot(a, b):
    return jnp.dot(a, b, preferred_element_type=F32)


def _head_rms(y):
    lo = _lo_half(y.shape)
    z = y * y
    sa = jnp.sum(jnp.where(lo, z, 0.0), axis=-1, keepdims=True)
    sb = jnp.sum(jnp.where(lo, 0.0, z), axis=-1, keepdims=True)
    inv = jnp.where(lo, lax.rsqrt(sa * (1.0 / HEAD_DIM) + EPS), lax.rsqrt(sb * (1.0 / HEAD_DIM) + EPS))
    return y * inv


def _proj_kernel(x_ref, g_ref, *rest, segs, cw):
    ns = len(segs)
    w_refs, s_refs, o_refs = rest[:ns], rest[ns:2 * ns], rest[2 * ns:3 * ns]
    n_scr = rest[3 * ns]
    x = x_ref[...]
    ms = jnp.mean(x * x, axis=-1, keepdims=True)
    n_scr[...] = (x * lax.rsqrt(ms + EPS) * g_ref[...]).astype(n_scr.dtype)
    for (cols, headnorm, transposed), w_ref, s_ref, o_ref in zip(segs, w_refs, s_refs, o_refs):
        for c0 in range(0, cols, cw):
            c1 = min(c0 + cw, cols)
            if transposed:
                yt = _dot_t(w_ref[c0:c1, :], n_scr[...])
                if headnorm:
                    heads = []
                    for r0 in range(0, c1 - c0, HEAD_DIM):
                        z = yt[r0:r0 + HEAD_DIM]
                        ss = jnp.sum(z * z, axis=0, keepdims=True)
                        heads.append(z * lax.rsqrt(ss * (1.0 / HEAD_DIM) + EPS))
                    yt = jnp.concatenate(heads, axis=0)
                o_ref[0, c0:c1, :] = (yt * s_ref[c0:c1, :]).astype(o_ref.dtype)
                continue
            y = _dot(n_scr[...], w_ref[:, c0:c1])
            for l0 in range(0, c1 - c0, LANE):
                yl = y[:, l0:l0 + LANE]
                if headnorm:
                    yl = _head_rms(yl)
                o_ref[:, c0 + l0:c0 + l0 + LANE] = (yl * s_ref[:, c0 + l0:c0 + l0 + LANE]).astype(o_ref.dtype)


def _in_proj(x, g, segs, batch, tm=512, cw=256):
    n, d = x.shape
    tpb = n // batch // tm
    ws = [(s[0].T if s[4] else s[0]).astype(BF16) for s in segs]
    ss = [s[1].astype(F32).reshape((-1, 1) if s[4] else (1, -1)) for s in segs]
    meta = tuple((int(s[0].shape[1]), bool(s[2]), bool(s[4])) for s in segs)
    in_specs = [pl.BlockSpec((tm, d), lambda i: (i, 0)), _const_spec((1, d))]
    in_specs += [_const_spec(w.shape) for w in ws] + [_const_spec(s.shape) for s in ss]
    out_specs = [pl.BlockSpec((1, c, tm), lambda i: (i // tpb, 0, i % tpb)) if t else pl.BlockSpec((tm, c), lambda i: (i, 0))
                 for c, _, t in meta]
    out_shape = [jax.ShapeDtypeStruct((batch, c, n // batch) if t else (n, c), s[3]) for (c, _, t), s in zip(meta, segs)]
    return pl.pallas_call(
        functools.partial(_proj_kernel, segs=meta, cw=cw),
        grid=(n // tm,),
        in_specs=in_specs,
        out_specs=out_specs,
        out_shape=out_shape,
        scratch_shapes=[pltpu.VMEM((tm, d), BF16)],
        compiler_params=_cparams(("arbitrary",)),
        name="in_proj",
    )(x, g.astype(F32).reshape(1, d), *ws, *ss)


def _toeplitz_kernel(vec_ref, off_ref, o_ref, *, rows, cols, band):
    n = vec_ref.shape[-1]
    x = jnp.broadcast_to(vec_ref[0], (rows, n))
    t = (pltpu.roll(x, 0, 1, stride=1, stride_axis=0)[:, :cols] - off_ref[0][:, :1]) * LOG2E
    if band is not None:
        r = lax.broadcasted_iota(I32, (rows, cols), 0)
        c = lax.broadcasted_iota(I32, (rows, cols), 1)
        lo = (r // CHUNK) * CHUNK
        t = jnp.where((c >= lo) & (c < lo + band), t, NEG)
    o_ref[0] = t


def _toeplitz(vec, off, rows, cols, band=None):
    h, n = vec.shape
    return pl.pallas_call(
        functools.partial(_toeplitz_kernel, rows=rows, cols=cols, band=band),
        grid=(h,),
        in_specs=[pl.BlockSpec((1, 1, n), lambda i: (i, 0, 0)), pl.BlockSpec((1, 1, LANE), lambda i: (i, 0, 0))],
        out_specs=pl.BlockSpec((1, rows, cols), lambda i: (i, 0, 0)),
        out_shape=jax.ShapeDtypeStruct((h, rows, cols), F32),
        compiler_params=_cparams(("arbitrary",)),
        name="toeplitz_bias",
    )(vec.reshape(h, 1, n), jnp.broadcast_to(off.reshape(h, 1, 1), (h, 1, LANE)))


def _t5_bucket_static(rel):
    n = np.abs(rel)
    large = 8 + sum((n >= t).astype(np.int64) for t in (12, 16, 23, 32, 46, 64, 91))
    return np.where(rel > 0, 16, 0) + np.where(n < 8, n, large)


def _bias_a(t5_table):
    n = 2 * TA
    j = np.arange(n)
    d = np.where(j < n // 2, j, j - n)
    bucket = _t5_bucket_static(-d - PAD_A)
    vec = t5_table[bucket, :].T.astype(F32)
    far = t5_table[T5_BUCKETS // 2 - 1, :].astype(F32)
    bias_max = jnp.maximum(jnp.max(t5_table.astype(F32) - far[None, :]), 0.0) * LOG2E
    return _toeplitz(vec, far, TA, QA), bias_max


def _bias_b(rel_table, q_g, k_g):
    n = 1024
    j = np.arange(n)
    d = np.where(j < WIN_B, j, j - n)
    idx = np.clip(PAD_B - d, -REL_CLIP, REL_CLIP) + REL_CLIP
    vec = rel_table[idx, :].T.astype(F32)
    top = (HEAD_DIM ** 0.5 * BOUND_SLACK) * jnp.max(jnp.abs(q_g)) * jnp.max(jnp.abs(k_g)) + jnp.maximum(jnp.max(rel_table), 0.0)
    return _toeplitz(vec, jnp.broadcast_to(top.astype(F32), (vec.shape[0],)), QB, WIN_B, band=PAD_B + CHUNK)


def _attn_b_kernel(q_ref, k_ref, v_ref, ok_ref, b_ref, o_ref):
    j = pl.program_id(2)

    def values(q0):
        return jnp.concatenate([v_ref[0, pl.ds(q0, WIN_B), :], ok_ref[pl.ds(q0, WIN_B), :]], axis=1)

    lo = _lo_half((QB, LANE))
    bias = b_ref[...].reshape(2 * QB, WIN_B)
    col = lax.broadcasted_iota(I32, (2 * QB, WIN_B), 1)

    def logits(t):
        q0 = pl.multiple_of(j * SB + t * QB, QB)
        qs = q_ref[0, pl.ds(pl.multiple_of(t * QB, QB), QB), :]
        zero = jnp.zeros_like(qs)
        qq = jnp.concatenate([jnp.where(lo, qs, zero), jnp.where(lo, zero, qs)], axis=0)
        return q0, _dot_t(qq, k_ref[0, pl.ds(q0, WIN_B), :]) + bias

    def store(t, pv):
        o = pv[:, :LANE] / pv[:, LANE:]
        o_ref[0, pl.ds(pl.multiple_of(t * QB, QB), QB), :] = jnp.where(lo, o[:QB], o[QB:]).astype(o_ref.dtype)

    def body(t2, carry):
        sums = None
        for k in range(UNROLL_B):
            t = t2 * UNROLL_B + k
            q0, s = logits(t)
            pv = _dot(jnp.exp2(s).astype(v_ref.dtype), values(q0))
            store(t, pv)
            sums = pv[:, LANE:] if sums is None else jnp.minimum(sums, pv[:, LANE:])

        @pl.when(jnp.min(sums) < SUM_FLOOR)
        def _():
            for k in range(UNROLL_B):
                t = t2 * UNROLL_B + k
                q0, s = logits(t)
                s = jnp.where(col >= PAD_B - q0, s, NEG)
                p = jnp.exp2(s - jnp.max(s, axis=-1, keepdims=True))
                store(t, _dot(p.astype(v_ref.dtype), values(q0)))

        return carry

    lax.fori_loop(0, SB // QB // UNROLL_B, body, 0)


def _attn_b(q, kp, vp, bias):
    b, s, d = q.shape
    npair = d // LANE
    ok = jnp.broadcast_to((jnp.arange(s + PAD_B) >= PAD_B).astype(vp.dtype)[:, None], (s + PAD_B, LANE))
    return pl.pallas_call(
        _attn_b_kernel,
        grid=(b, npair, s // SB),
        in_specs=[
            pl.BlockSpec((1, SB, LANE), lambda bi, p, j: (bi, j, p)),
            pl.BlockSpec((1, s + PAD_B, LANE), lambda bi, p, j: (bi, 0, p)),
            pl.BlockSpec((1, s + PAD_B, LANE), lambda bi, p, j: (bi, 0, p)),
            _const_spec((s + PAD_B, LANE)),
            pl.BlockSpec((2, QB, WIN_B), lambda bi, p, j: (p, 0, 0)),
        ],
        out_specs=pl.BlockSpec((1, SB, LANE), lambda bi, p, j: (bi, j, p)),
        out_shape=jax.ShapeDtypeStruct((b, s, d), q.dtype),
        compiler_params=_cparams(("arbitrary", "arbitrary", "arbitrary")),
        name="attn_band",
    )(q, kp, vp, ok, bias)


KEY_FINITE_MAX = 0x7F7FFFFF
KEY_FINITE_MIN = -0x7F800000


def _score_to_key(x):
    bits = lax.bitcast_convert_type(x, I32)
    return bits ^ ((bits >> 31) & 0x7FFFFFFF)


def _key_to_score(k):
    k = jnp.minimum(jnp.maximum(k, KEY_FINITE_MIN), KEY_FINITE_MAX)
    return lax.bitcast_convert_type(k ^ ((k >> 31) & 0x7FFFFFFF), F32)


def _count(keys, start, ntile, pred):
    nacc = 4

    def tile(t, accs):
        accs = list(accs)
        base = start + t * TA
        for cb in range(TA // CB):
            blk = keys[pl.ds(base + cb * CB, CB), :]
            for c in range(CB // SUBLANE):
                kk = blk[c * SUBLANE:(c + 1) * SUBLANE]
                a = accs[c % nacc]
                accs[c % nacc] = jnp.where(pred(kk, base + cb * CB + c * SUBLANE), a + 1, a)
        return tuple(accs)

    accs = lax.fori_loop(0, ntile, tile, tuple(jnp.zeros((SUBLANE, QA), I32) for _ in range(nacc)))
    tot = (accs[0] + accs[1]) + (accs[2] + accs[3])
    return jnp.sum(tot, axis=0, keepdims=True).astype(F32)


def _snap(keys, start, ntile, lo, hi):
    def tile(t, accs):
        mn, mx = list(accs[:2]), list(accs[2:])
        base = start + t * TA
        for cb in range(TA // CB):
            blk = keys[pl.ds(base + cb * CB, CB), :]
            for c in range(CB // SUBLANE):
                kk = blk[c * SUBLANE:(c + 1) * SUBLANE]
                mn[c % 2] = jnp.minimum(mn[c % 2], jnp.where(kk >= lo, kk, INT_MAX))
                mx[c % 2] = jnp.maximum(mx[c % 2], jnp.where(kk < hi, kk, INT_MIN))
        return tuple(mn + mx)

    top = jnp.full((SUBLANE, QA), INT_MAX, I32)
    bot = jnp.full((SUBLANE, QA), INT_MIN, I32)
    mn0, mn1, mx0, mx1 = lax.fori_loop(0, ntile, tile, (top, top, bot, bot))
    return (jnp.min(jnp.minimum(mn0, mn1), axis=0, keepdims=True),
            jnp.max(jnp.maximum(mx0, mx1), axis=0, keepdims=True))


def _demote_ties(keys, start, ntile, thr, keep, tie):
    rid = lax.broadcasted_iota(I32, (SUBLANE, QA), 0)

    def block(c, seen):
        r0 = start + c * CB
        blk = keys[pl.ds(r0, CB), :]
        out = []
        for s in range(CB // SUBLANE):
            kk = blk[s * SUBLANE:(s + 1) * SUBLANE]
            eq = (kk == thr) & tie
            e = jnp.where(eq, 1, 0)
            for sh in (1, 2, 4):
                e = e + jnp.where(rid >= sh, pltpu.roll(e, sh, 0), 0)
            out.append(jnp.where(eq & (seen + e > keep), INT_MIN, kk))
            seen = seen + e[SUBLANE - 1:SUBLANE]
        keys[pl.ds(r0, CB), :] = jnp.concatenate(out, axis=0)
        return seen

    lax.fori_loop(0, ntile * (TA // CB), block, jnp.zeros((1, QA), I32))


def _attn_a_kernel(qt_ref, qit_ref, wit_ref, k_ref, ki_ref, vt_ref, b_ref, top_ref, o_ref,
                   keys, x0_scr, x1_scr, s_scr, p_scr, qs_scr, qis_scr, acc_scr, m_scr, al_scr, mb_scr, gm_scr,
                   lo_scr, hi_scr, cl_scr, ch_scr):
    i = pl.program_id(1)
    topk = float(TOPK_MAX)
    sub_lo = lax.broadcasted_iota(I32, (LANE, QA), 0) < HEAD_DIM

    for t in range(8):
        qt = qt_ref[0, t * LANE:(t + 1) * LANE, :]
        gp, r = divmod(t, 4)
        zero = jnp.zeros_like(qt)
        qs_scr[2 * gp, :, r * QA:(r + 1) * QA] = jnp.where(sub_lo, qt, zero)
        qs_scr[2 * gp + 1, :, r * QA:(r + 1) * QA] = jnp.where(sub_lo, zero, qt)
    for p in range(IDX_HEADS // 2):
        qt = qit_ref[0, p * LANE:(p + 1) * LANE, :]
        zero = jnp.zeros_like(qt)
        qis_scr[:, (2 * p) * QA:(2 * p + 1) * QA] = jnp.where(sub_lo, qt, zero)
        qis_scr[:, (2 * p + 1) * QA:(2 * p + 2) * QA] = jnp.where(sub_lo, zero, qt)

    gm_scr[...] = jnp.full((TI, QA), INT_MIN, I32)
    qcol = lax.broadcasted_iota(I32, (CB, QA), 1)
    krow = lax.broadcasted_iota(I32, (CB, QA), 0)
    last = i + 1
    x_scr = (x0_scr, x1_scr)

    def score_dot(tt, buf):
        x_scr[buf][...] = _dot(ki_ref[0, pl.ds(pl.multiple_of(tt * TI, TI), TI), :], qis_scr[...])

    def score_keys(tt, buf):
        base = pl.multiple_of(tt * TI, TI)
        for cb in range(TI // CB):
            sc = jnp.zeros((CB, QA), F32)
            for h in range(IDX_HEADS):
                sc = sc + wit_ref[0, h:h + 1, :] * jnp.maximum(x_scr[buf][cb * CB:(cb + 1) * CB, h * QA:(h + 1) * QA], 0.0)
            causal = (krow + cb * CB < (qcol // CHUNK + 1) * CHUNK) | (tt < last)
            key = jnp.where(causal & (tt > 0), _score_to_key(sc), INT_MIN)
            keys[pl.ds(base + cb * CB, CB), :] = key
            gm_scr[cb * CB:(cb + 1) * CB, :] = jnp.maximum(gm_scr[cb * CB:(cb + 1) * CB, :], key)

    first = i & 1
    score_dot(first, 0)

    def score_pair(p, carry):
        ta = first + 2 * p
        score_dot(ta + 1, 1)
        score_keys(ta, 0)
        score_dot(jnp.minimum(ta + 2, last), 0)
        score_keys(ta + 1, 1)
        return carry

    lax.fori_loop(0, (i + 2 - first) // 2, score_pair, 0)

    gm = gm_scr[...]
    s_start = pl.multiple_of((i & 1) * PAD_A, PAD_A)
    s_ntile = (i + 2) // 2
    lo0 = jnp.min(gm, axis=0, keepdims=True)
    lo_scr[...] = lo0
    hi_scr[...] = jnp.max(gm, axis=0, keepdims=True) + 1
    cl_scr[...] = _count(keys, s_start, s_ntile, lambda kk, _: kk >= lo0)
    ch_scr[...] = jnp.zeros((1, QA), F32)

    def open_span():
        opn = hi_scr[...] - 1 > lo_scr[...]
        return jnp.max(jnp.where(opn, jnp.maximum(cl_scr[...] - ch_scr[...], 1.0), 0.0))

    def search(state):
        span, prev = state

        @pl.when((span <= SNAP_SPAN) | (span >= prev))
        def _():
            lo, hi = lo_scr[...], hi_scr[...]
            amin, amax = _snap(keys, s_start, s_ntile, lo, hi)
            opn = hi - 1 > lo
            lo_scr[...] = jnp.where(opn, amin, lo)
            hi_scr[...] = jnp.where(opn, amax + 1, hi)

        for by_value in (True, False):
            lo, hi = lo_scr[...], hi_scr[...]
            cl, ch = cl_scr[...], ch_scr[...]
            if by_value:
                fm = 0.5 * _key_to_score(lo) + 0.5 * _key_to_score(hi)
                mid = jnp.where(hi - 1 > lo, jnp.minimum(jnp.maximum(_score_to_key(fm), lo + 1), hi - 1), lo)
            else:
                mid = (lo >> 1) + (hi >> 1) + (lo & hi & 1)
            cnt = _count(keys, s_start, s_ntile, lambda kk, _: kk >= mid)
            ge = cnt >= topk
            exact = cnt == topk
            lo_scr[...] = jnp.where(ge, mid, lo)
            cl_scr[...] = jnp.where(ge, cnt, cl)
            hi_scr[...] = jnp.where(exact, mid + 1, jnp.where(ge, hi, mid))
            ch_scr[...] = jnp.where(ge, ch, cnt)
        return open_span(), span

    lax.while_loop(lambda st: st[0] > 0.0, search, (open_span(), jnp.float32(3e38)))

    thr = lo_scr[...]
    tie = (cl_scr[...] > topk) & (thr > INT_MIN)

    @pl.when(jnp.max(jnp.where(tie, 1.0, 0.0)) > 0.0)
    def _():
        _demote_ties(keys, s_start, s_ntile, thr, (topk - ch_scr[...]).astype(I32), tie)

    thr = jnp.maximum(thr, KEY_FINITE_MIN)

    top = top_ref[0:1, 0:1]

    def attend(start, with_bias, running_max):
        shift = 0.0 if running_max else -top
        for cb in range(TA // CB):
            rows = pl.ds(start + cb * CB, CB)
            kk = keys[rows, :]
            mb_scr[cb * CB:(cb + 1) * CB, :] = jnp.where((kk >= thr) & (kk <= KEY_FINITE_MAX), shift, NEG)
        blk = start // TI

        def qk(g):
            s_scr[g % 2] = _dot(k_ref[0, pl.ds(start, TA), (g // 2) * LANE:(g // 2 + 1) * LANE], qs_scr[g])

        qk(0)
        for g in range(KV_GROUPS):
            buf = g % 2
            if g + 1 < KV_GROUPS:
                qk(g + 1)
            hds = [g * Q_PER_KV + r for r in range(Q_PER_KV)]

            def logits(kb):
                rows = slice(kb * KB, (kb + 1) * KB)
                mb = mb_scr[rows, :]
                out = []
                for r in range(Q_PER_KV):
                    s = s_scr[buf, rows, r * QA:(r + 1) * QA] + mb
                    if with_bias:
                        s = s + b_ref[hds[r], rows, :]
                    out.append(s)
                return out

            if running_max:
                mx = logits(0)
                for kb in range(1, TA // KB):
                    mx = [jnp.maximum(a, s) for a, s in zip(mx, logits(kb))]
                m_new = []
                for r in range(Q_PER_KV):
                    m_old = m_scr[hds[r]]
                    mn = jnp.maximum(m_old, jnp.max(mx[r], axis=0, keepdims=True))
                    al_scr[hds[r]] = jnp.exp2(m_old - mn)
                    m_scr[hds[r]] = mn
                    m_new.append(jnp.broadcast_to(mn, (KB, QA)))
            for kb in range(TA // KB):
                ss = logits(kb)
                for r in range(Q_PER_KV):
                    t = ss[r] - m_new[r] if running_max else ss[r]
                    p_scr[buf, kb * KB:(kb + 1) * KB, r * QA:(r + 1) * QA] = jnp.exp2(t).astype(p_scr.dtype)
            for r in range(Q_PER_KV):
                cols = slice(r * QA, (r + 1) * QA)
                pv = _dot(vt_ref[0, g, blk], p_scr[buf, 0:TI, cols]) + _dot(vt_ref[0, g, blk + 1], p_scr[buf, TI:TA, cols])
                if running_max:
                    acc_scr[hds[r]] = al_scr[hds[r]] * acc_scr[hds[r]] + pv
                else:
                    acc_scr[hds[r]] = acc_scr[hds[r]] + pv

    def sweep(running_max):
        acc_scr[...] = jnp.zeros(acc_scr.shape, F32)

        def far_body(t, carry):
            attend(pl.multiple_of((i & 1) * PAD_A + t * TA, PAD_A), False, running_max)
            return carry

        lax.fori_loop(0, i // 2, far_body, 0)
        attend(pl.multiple_of(i * QA, QA), True, running_max)

    sweep(False)
    lmin = acc_scr[0, HEAD_DIM:HEAD_DIM + 1, :]
    for hd in range(1, acc_scr.shape[0]):
        lmin = jnp.minimum(lmin, acc_scr[hd, HEAD_DIM:HEAD_DIM + 1, :])

    @pl.when(jnp.min(lmin) < SUM_FLOOR)
    def _():
        m_scr[...] = jnp.full(m_scr.shape, NEG, F32)
        sweep(True)

    for t in range(8):
        gp, r = divmod(t, 4)
        aa = acc_scr[(2 * gp) * Q_PER_KV + r]
        ab = acc_scr[(2 * gp + 1) * Q_PER_KV + r]
        oa = aa[:HEAD_DIM] / aa[HEAD_DIM:HEAD_DIM + 1]
        ob = ab[:HEAD_DIM] / ab[HEAD_DIM:HEAD_DIM + 1]
        o_ref[0, :, t * LANE:(t + 1) * LANE] = jnp.concatenate([oa, ob], axis=0).T.astype(o_ref.dtype)


def _attn_a(qt, qit, wit, kp, kip, vt, bias, top):
    b, d, s = qt.shape
    sp = s + PAD_A
    nh = d // HEAD_DIM
    return pl.pallas_call(
        _attn_a_kernel,
        grid=(b, s // QA),
        in_specs=[
            pl.BlockSpec((1, d, QA), lambda bi, i: (bi, 0, i)),
            pl.BlockSpec((1, qit.shape[1], QA), lambda bi, i: (bi, 0, i)),
            pl.BlockSpec((1, IDX_HEADS, QA), lambda bi, i: (bi, 0, i)),
            pl.BlockSpec((1, sp, kp.shape[-1]), lambda bi, i: (bi, 0, 0), pipeline_mode=pl.Buffered(1)),
            pl.BlockSpec((1, sp, LANE), lambda bi, i: (bi, 0, 0), pipeline_mode=pl.Buffered(1)),
            pl.BlockSpec((1,) + vt.shape[1:], lambda bi, i: (bi, 0, 0, 0, 0), pipeline_mode=pl.Buffered(1)),
            _const_spec(bias.shape),
            _const_spec(top.shape),
        ],
        out_specs=pl.BlockSpec((1, QA, d), lambda bi, i: (bi, i, 0)),
        out_shape=jax.ShapeDtypeStruct((b, s, d), qt.dtype),
        scratch_shapes=[
            pltpu.VMEM((sp, QA), I32),
            pltpu.VMEM((TI, IDX_HEADS * QA), F32),
            pltpu.VMEM((TI, IDX_HEADS * QA), F32),
            pltpu.VMEM((2, TA, Q_PER_KV * QA), F32),
            pltpu.VMEM((2, TA, Q_PER_KV * QA), qt.dtype),
            pltpu.VMEM((KV_GROUPS, LANE, Q_PER_KV * QA), qt.dtype),
            pltpu.VMEM((LANE, IDX_HEADS * QA), qit.dtype),
            pltpu.VMEM((nh, VR, QA), F32),
            pltpu.VMEM((nh, 1, QA), F32),
            pltpu.VMEM((nh, 1, QA), F32),
            pltpu.VMEM((TA, QA), F32),
            pltpu.VMEM((TI, QA), I32),
            pltpu.VMEM((1, QA), I32),
            pltpu.VMEM((1, QA), I32),
            pltpu.VMEM((1, QA), F32),
            pltpu.VMEM((1, QA), F32),
        ],
        compiler_params=_cparams(("arbitrary", "arbitrary")),
        name="attn_sparse",
    )(qt, qit, wit, kp, kip, vt, bias, top)


HALO = 16


def _ffn_kernel(h_ref, hh_ref, m_ref, mh_ref, wo_ref, g_ref, wup_ref, cw_ref, cb_ref, wdn_ref, o_ref,
                me_scr, n_scr, h1_scr, u_scr, acc_scr, *, tm, dff, cw, seq):
    i = pl.program_id(0)
    first = (i * tm) % seq == 0
    me_scr[0:HALO] = mh_ref[...]
    me_scr[HALO:] = m_ref[...]
    h1_scr[...] = _dot(me_scr[...], wo_ref[...])
    h1_scr[0:HALO] = h1_scr[0:HALO] + hh_ref[...]
    h1_scr[HALO:] = h1_scr[HALO:] + h_ref[...]
    h1 = h1_scr[...]
    ms = jnp.mean(h1 * h1, axis=-1, keepdims=True)
    n = h1 * lax.rsqrt(ms + EPS) * g_ref[...]
    row = lax.broadcasted_iota(I32, n.shape, 0)
    n_scr[...] = jnp.where((row < HALO) & first, 0.0, n).astype(n_scr.dtype)
    acc_scr[...] = jnp.zeros(acc_scr.shape, F32)

    def up(c):
        for part in range(2):
            c0 = part * dff + c * cw
            u_scr[c % 2, part] = _dot(n_scr[...], wup_ref[:, c0:c0 + cw])

    up(0)
    for c in range(dff // cw):
        if c + 1 < dff // cw:
            up(c + 1)
        ys = []
        for part in range(2):
            c0 = part * dff + c * cw
            u = u_scr.at[c % 2, part]
            y = cb_ref[:, c0:c0 + cw] + cw_ref[0:1, c0:c0 + cw] * u[HALO - 2:HALO - 2 + tm, :]
            y = y + cw_ref[1:2, c0:c0 + cw] * u[HALO - 1:HALO - 1 + tm, :]
            y = y + cw_ref[2:3, c0:c0 + cw] * u[HALO:HALO + tm, :]
            ys.append(y)
        a, gte = ys
        act = (gte * (1.0 / (1.0 + jnp.exp(-gte)))) * a
        acc_scr[...] += _dot(act.astype(n_scr.dtype), wdn_ref[c * cw:(c + 1) * cw, :])
    o_ref[...] = h1_scr[HALO:] + acc_scr[...]


def _out_ffn(h, m, w_out, g, w_up, conv_w, conv_b, w_down, seq, tm=512, cw=256):
    n, d = h.shape
    dff = w_down.shape[0]
    hb = tm // HALO
    halo = lambda i: (jnp.maximum(i * hb - 1, 0), 0)
    return pl.pallas_call(
        functools.partial(_ffn_kernel, tm=tm, dff=dff, cw=cw, seq=seq),
        grid=(n // tm,),
        in_specs=[
            pl.BlockSpec((tm, d), lambda i: (i, 0)),
            pl.BlockSpec((HALO, d), halo),
            pl.BlockSpec((tm, d), lambda i: (i, 0)),
            pl.BlockSpec((HALO, d), halo),
            _const_spec((d, d)),
            _const_spec((1, d)),
            _const_spec((d, 2 * dff)),
            _const_spec((3, 2 * dff)),
            _const_spec((1, 2 * dff)),
            _const_spec((dff, d)),
        ],
        out_specs=pl.BlockSpec((tm, d), lambda i: (i, 0)),
        out_shape=jax.ShapeDtypeStruct((n, d), F32),
        scratch_shapes=[
            pltpu.VMEM((tm + HALO, d), BF16),
            pltpu.VMEM((tm + HALO, d), BF16),
            pltpu.VMEM((tm + HALO, d), F32),
            pltpu.VMEM((2, 2, tm + HALO, cw), F32),
            pltpu.VMEM((tm, d), F32),
        ],
        compiler_params=_cparams(("arbitrary",)),
        name="out_ffn",
    )(h, h, m, m, w_out.astype(BF16), g.astype(F32).reshape(1, d), w_up.astype(BF16),
      conv_w.astype(F32), conv_b.astype(F32).reshape(1, -1), w_down.astype(BF16))


_HEAD_PERM = np.array([8 * gp + 4 * half + r for gp in range(2) for r in range(4) for half in range(2)])
_COL_PERM = (_HEAD_PERM[:, None] * HEAD_DIM + np.arange(HEAD_DIM)[None, :]).reshape(-1)


def _pad_front(x, b, s, pad):
    return jnp.pad(x.reshape(b, s, x.shape[-1]), ((0, 0), (pad, 0), (0, 0)))


def _mixer_a(h2, b, s, g_attn, w_in, q_g, k_g, bias, bias_max):
    ad = Q_PER_KV * KV_GROUPS * HEAD_DIM
    kd = KV_GROUPS * HEAD_DIM
    o_q, o_k, o_v, o_qi = 0, ad, ad + kd, ad + 2 * kd
    o_ki = o_qi + IDX_HEADS * IDX_DIM
    o_wi = o_ki + IDX_DIM
    w_ki = w_in[:, o_ki:o_wi]
    wi_scale = IDX_HEADS ** -0.5 * IDX_DIM ** -0.5
    ones = lambda c: jnp.ones((c,), F32)
    segs = [
        (w_in[:, o_q:o_k][:, _COL_PERM], jnp.tile(q_g, ad // HEAD_DIM) * (HEAD_DIM ** -0.5 * LOG2E), True, BF16, True),
        (w_in[:, o_k:o_v], jnp.tile(k_g, KV_GROUPS), True, BF16, False),
        (w_in[:, o_v:o_qi], ones(kd), False, BF16, True),
        (w_in[:, o_qi:o_ki], ones(IDX_HEADS * IDX_DIM), False, BF16, True),
        (jnp.concatenate([w_ki, w_ki], axis=1), ones(LANE), True, BF16, False),
        (w_in[:, o_wi:o_wi + IDX_HEADS], ones(IDX_HEADS) * wi_scale, False, F32, True),
    ]
    qt, k, vt, qit, ki, wit = _in_proj(h2, g_attn, segs, b)
    sp = s + PAD_A
    vt = jnp.pad(vt, ((0, 0), (0, 0), (PAD_A, 0))).reshape(b, KV_GROUPS, HEAD_DIM, sp)
    vt = jnp.concatenate([vt, jnp.ones((b, KV_GROUPS, VR - HEAD_DIM, sp), vt.dtype)], axis=2)
    vt = vt.reshape(b, KV_GROUPS, VR, sp // TI, TI).transpose(0, 1, 3, 2, 4)
    top = (HEAD_DIM ** 0.5 * LOG2E * BOUND_SLACK) * jnp.max(jnp.abs(q_g)) * jnp.max(jnp.abs(k_g)) + bias_max
    o = _attn_a(qt, qit, wit, _pad_front(k, b, s, PAD_A), _pad_front(ki, b, s, PAD_A),
                vt, bias, jnp.broadcast_to(top.astype(F32), (1, LANE)))
    return o.reshape(b * s, ad)


def _mixer_b(h2, b, s, g_attn, w_in, q_g, k_g, bias):
    d = w_in.shape[1] // 3
    nh = d // HEAD_DIM
    segs = [
        (w_in[:, :d], jnp.tile(q_g, nh) * (HEAD_DIM ** -0.5 * LOG2E), True, BF16, False),
        (w_in[:, d:2 * d], jnp.tile(k_g, nh), True, BF16, False),
        (w_in[:, 2 * d:], jnp.ones((d,), F32), False, BF16, False),
    ]
    q, k, v = _in_proj(h2, g_attn, segs, b)
    o = _attn_b(q.reshape(b, s, d), _pad_front(k, b, s, PAD_B), _pad_front(v, b, s, PAD_B), bias)
    return o.reshape(b * s, d)


def kernel(x, attn_norm_g, w_in_a, w_in_b, q_norm_g, k_norm_g, t5_bias, rel_bias_b, w_out, ffn_norm_g, w_up, conv_w, conv_b, w_down):
    b, s, d = x.shape
    depth = attn_norm_g.shape[0]
    assert s % SB == 0 and s % QA == 0 and min(TOPK_MAX, s // 4) == TOPK_MAX
    h = x.reshape(b * s, d)
    bias_a, bias_a_max = _bias_a(t5_bias)
    for i in range(depth):
        if i % 2 == 0:
            m = _mixer_a(h, b, s, attn_norm_g[i], w_in_a[i // 2], q_norm_g[i], k_norm_g[i], bias_a, bias_a_max)
            wo = w_out[i][_COL_PERM, :]
        else:
            m = _mixer_b(h, b, s, attn_norm_g[i], w_in_b[i // 2], q_norm_g[i], k_norm_g[i],
                         _bias_b(rel_bias_b[i // 2], q_norm_g[i], k_norm_g[i]))
            wo = w_out[i]
        h = _out_ffn(h, m, wo, ffn_norm_g[i], w_up[i], conv_w[i], conv_b[i], w_down[i], s)
    return h.reshape(b, s, d)
```
